```python
import math
import jax, jax.numpy as jnp
from jax import lax
import numpy as np

D_MODEL = 1024
BATCH = 8
SEQ = 4096
DEPTH = 2

MIX_WIDTH = 2 * D_MODEL
GROUP_WIDTH = MIX_WIDTH // 4
MEM_LEN = 256
EPS = 1e-6

MOBA_HEADS = 8
MOBA_HEAD_DIM = GROUP_WIDTH // MOBA_HEADS
MOBA_BLOCK = 256
MOBA_TOPK = 3
MOBA_Q_CHUNK = 16

NSA_HEADS = 8
NSA_KV_HEADS = 2
NSA_HEAD_DIM = GROUP_WIDTH // NSA_HEADS
NSA_KV_WIDTH = NSA_KV_HEADS * NSA_HEAD_DIM
NSA_CMP_LEN = 32
NSA_CMP_STRIDE = 16
NSA_CMP_HIDDEN = 128
NSA_SLC_BLOCK = 64
NSA_SLC_TOPK = 16
NSA_WINDOW = 512
NSA_Q_CHUNK = 64

RET_HEADS = 4
RET_KEY_DIM = 64
RET_VAL_DIM = GROUP_WIDTH // RET_HEADS
RET_QK_WIDTH = RET_HEADS * RET_KEY_DIM
RET_CHUNK = 128

MEM_HEADS = 4
MEM_HEAD_DIM = GROUP_WIDTH // MEM_HEADS

IN_SPLITS = (
    GROUP_WIDTH, GROUP_WIDTH, GROUP_WIDTH,
    GROUP_WIDTH,
    NSA_KV_WIDTH, NSA_KV_WIDTH,
    NSA_KV_WIDTH, NSA_KV_WIDTH,
    NSA_KV_WIDTH, NSA_KV_WIDTH,
    3 * NSA_HEADS,
    RET_QK_WIDTH, RET_QK_WIDTH, GROUP_WIDTH,
    GROUP_WIDTH,
    MIX_WIDTH,
)
IN_COLS = sum(IN_SPLITS)

kernel_name = "hybrid_moba_nsa_retention_block"


def rms_norm(x, g):
    xf = x.astype(jnp.float32)
    y = xf * lax.rsqrt(jnp.mean(xf * xf, axis=-1, keepdims=True) + EPS)
    return (y * g.astype(jnp.float32)).astype(x.dtype)


def split_cols(t, sizes):
    offs = np.cumsum(np.array(sizes))[:-1].tolist()
    return jnp.split(t, offs, axis=-1)


def masked_softmax(s, mask):
    s = jnp.where(mask, s.astype(jnp.float32), -jnp.inf)
    m = jnp.max(s, axis=-1, keepdims=True)
    m = jnp.where(jnp.isfinite(m), m, 0.0)
    p = jnp.exp(s - m)
    den = jnp.sum(p, axis=-1, keepdims=True)
    return p / jnp.where(den > 0, den, 1.0)


def moba_attention(q, k, v):
    B, S, H, Dh = q.shape
    nb = -(-S // MOBA_BLOCK)
    pad = nb * MOBA_BLOCK - S
    top = min(MOBA_TOPK, nb)
    qc_len = MOBA_Q_CHUNK
    nc = S // qc_len
    scale = Dh ** -0.5
    kp = jnp.pad(k, ((0, 0), (0, pad), (0, 0), (0, 0)))
    vp = jnp.pad(v, ((0, 0), (0, pad), (0, 0), (0, 0)))
    kb = kp.reshape(B, nb, MOBA_BLOCK, H, Dh).transpose(0, 3, 1, 2, 4)
    vb = vp.reshape(B, nb, MOBA_BLOCK, H, Dh).transpose(0, 3, 1, 2, 4)
    k_mean = jnp.mean(kb.astype(jnp.float32), axis=3)
    own = jnp.arange(S) // MOBA_BLOCK
    past = jnp.arange(nb)[None, :] < own[:, None]
    gate = jnp.einsum('bshd,bhnd->bhsn', q.astype(jnp.float32), k_mean)
    _, sel = lax.top_k(jnp.where(past, gate, -jnp.inf), top)
    valid = sel < own[None, None, :, None]

    q_ch = q.reshape(B, nc, qc_len, H, Dh).transpose(1, 0, 2, 3, 4)
    sel_ch = sel.reshape(B, H, nc, qc_len, top).transpose(2, 0, 1, 3, 4)
    val_ch = valid.reshape(B, H, nc, qc_len, top).transpose(2, 0, 1, 3, 4)
    starts = jnp.arange(nc, dtype=jnp.int32) * qc_len
    bi = jnp.arange(B)[:, None, None, None]
    hi = jnp.arange(H)[None, :, None, None]

    def chunk(args):
        qc, sc, vc_ok, t0 = args
        tq = t0 + jnp.arange(qc_len)
        ks = kb[bi, hi, sc]
        vs = vb[bi, hi, sc]
        s_sel = jnp.einsum('bqhd,bhqjkd->bhqjk', qc, ks) * scale
        m_sel = jnp.broadcast_to(vc_ok[..., None], s_sel.shape)
        n0 = (t0 // MOBA_BLOCK) * MOBA_BLOCK
        ko = lax.dynamic_slice_in_dim(kp, n0, MOBA_BLOCK, axis=1)
        vo = lax.dynamic_slice_in_dim(vp, n0, MOBA_BLOCK, axis=1)
        s_own = jnp.einsum('bqhd,bkhd->bhqk', qc, ko) * scale
        m_own = (n0 + jnp.arange(MOBA_BLOCK))[None, :] <= tq[:, None]
        n_sel = top * MOBA_BLOCK
        s = jnp.concatenate([s_sel.reshape(B, H, qc_len, n_sel), s_own], axis=-1)
        m = jnp.concatenate([m_sel.reshape(B, H, qc_len, n_sel),
                             jnp.broadcast_to(m_own, (B, H, qc_len, MOBA_BLOCK))], axis=-1)
        p = masked_softmax(s, m).astype(v.dtype)
        p_sel = p[..., :n_sel].reshape(B, H, qc_len, top, MOBA_BLOCK)
        p_own = p[..., n_sel:]
        return (jnp.einsum('bhqjk,bhqjkd->bqhd', p_sel, vs)
                + jnp.einsum('bhqk,bkhd->bqhd', p_own, vo))

    o = lax.map(chunk, (q_ch, sel_ch, val_ch, starts))
    return o.transpose(1, 0, 2, 3, 4).reshape(B, S, H * Dh).astype(q.dtype)


def _cmp_to_slc_matrix(n_cmp, n_slc):
    rs = NSA_SLC_BLOCK // NSA_CMP_STRIDE
    rc = NSA_CMP_LEN // NSA_CMP_STRIDE
    j = np.arange(n_slc)[:, None, None]
    i = np.broadcast_to(rs * j + np.arange(rs)[None, :, None] - np.arange(rc)[None, None, :], (n_slc, rs, rc))
    jj = np.broadcast_to(j, i.shape)
    ok = (i >= 0) & (i < n_cmp)
    mat = np.zeros((n_cmp, n_slc), np.float32)
    np.add.at(mat, (i[ok], jj[ok]), 1.0)
    return jnp.asarray(mat)


def nsa_attention(q, k_cmp, v_cmp, k_slc, v_slc, k_win, v_win, gate_logits,
                  pe_k, w1_k, w2_k, pe_v, w1_v, w2_v):
    B, S, H, Dh = q.shape
    G = k_cmp.shape[2]
    P = H // G
    scale = Dh ** -0.5
    n_cmp = (S - NSA_CMP_LEN) // NSA_CMP_STRIDE + 1
    n_slc = S // NSA_SLC_BLOCK
    top = min(NSA_SLC_TOPK, n_slc)
    qc_len = NSA_Q_CHUNK
    nc = S // qc_len
    W = NSA_WINDOW

    cmp_idx = NSA_CMP_STRIDE * np.arange(n_cmp)[:, None] + np.arange(NSA_CMP_LEN)[None, :]

    def compress(t, pe, w1, w2):
        blk = t[:, cmp_idx] + pe[None, None, :, None, :]
        flat = blk.transpose(0, 1, 3, 2, 4).reshape(B, n_cmp, G, NSA_CMP_LEN * Dh)
        return jax.nn.silu(flat @ w1) @ w2

    kc = compress(k_cmp, pe_k, w1_k, w2_k)
    vc = compress(v_cmp, pe_v, w1_v, w2_v)
    cmp_end = NSA_CMP_STRIDE * jnp.arange(n_cmp) + NSA_CMP_LEN - 1
    cmp_to_slc = _cmp_to_slc_matrix(n_cmp, n_slc)

    kb = k_slc.reshape(B, n_slc, NSA_SLC_BLOCK, G, Dh).transpose(0, 3, 1, 2, 4)
    vb = v_slc.reshape(B, n_slc, NSA_SLC_BLOCK, G, Dh).transpose(0, 3, 1, 2, 4)
    kw = jnp.pad(k_win, ((0, 0), (W, 0), (0, 0), (0, 0)))
    vw = jnp.pad(v_win, ((0, 0), (W, 0), (0, 0), (0, 0)))
    gates = jax.nn.sigmoid(gate_logits.astype(jnp.float32)).reshape(B, S, G, P, 3)

    q_ch = q.reshape(B, nc, qc_len, G, P, Dh).transpose(1, 0, 2, 3, 4, 5)
    g_ch = gates.reshape(B, nc, qc_len, G, P, 3).transpose(1, 0, 2, 3, 4, 5)
    starts = jnp.arange(nc, dtype=jnp.int32) * qc_len
    bi = jnp.arange(B)[:, None, None, None]
    gi = jnp.arange(G)[None, :, None, None]
    blk_ids = jnp.arange(n_slc)

    def chunk(args):
        qc, gc, t0 = args
        tq = t0 + jnp.arange(qc_len)
        s = jnp.einsum('bqgpd,bngd->bgpqn', qc, kc) * scale
        p_cmp = masked_softmax(s, cmp_end[None, :] <= tq[:, None])
        o_cmp = jnp.einsum('bgpqn,bngd->bqgpd', p_cmp.astype(vc.dtype), vc)
        imp = jnp.einsum('bgpqn,ns->bgqs', p_cmp, cmp_to_slc)
        own = tq // NSA_SLC_BLOCK
        forced = ((blk_ids[None, :] == 0) | (blk_ids[None, :] == own[:, None])
                  | (blk_ids[None, :] == own[:, None] - 1))
        imp = jnp.where(forced, jnp.inf, imp)
        imp = jnp.where(blk_ids[None, :] <= own[:, None], imp, -jnp.inf)
        _, sel = lax.top_k(imp, top)
        ks = kb[bi, gi, sel]
        vs = vb[bi, gi, sel]
        kpos = (sel[..., None] * NSA_SLC_BLOCK + jnp.arange(NSA_SLC_BLOCK)).reshape(
            B, G, 1, qc_len, top * NSA_SLC_BLOCK)
        s = jnp.einsum('bqgpd,bgqjkd->bgpqjk', qc, ks).reshape(
            B, G, P, qc_len, top * NSA_SLC_BLOCK) * scale
        p = masked_softmax(s, kpos <= tq[:, None]).astype(vs.dtype)
        o_slc = jnp.einsum('bgpqjk,bgqjkd->bqgpd',
                           p.reshape(B, G, P, qc_len, top, NSA_SLC_BLOCK), vs)
        kwc = lax.dynamic_slice_in_dim(kw, t0, W + qc_len, axis=1)
        vwc = lax.dynamic_slice_in_dim(vw, t0, W + qc_len, axis=1)
        kp = t0 - W + jnp.arange(W + qc_len)
        wmask = ((kp[None, :] <= tq[:, None]) & (kp[None, :] > tq[:, None] - W)
                 & (kp[None, :] >= 0))
        s = jnp.einsum('bqgpd,bkgd->bgpqk', qc, kwc) * scale
        p = masked_softmax(s, wmask).astype(vwc.dtype)
        o_win = jnp.einsum('bgpqk,bkgd->bqgpd', p, vwc)
        return gc[..., 0:1] * o_cmp + gc[..., 1:2] * o_slc + gc[..., 2:3] * o_win

    o = lax.map(chunk, (q_ch, g_ch, starts))
    return o.transpose(1, 0, 2, 3, 4, 5).reshape(B, S, H * Dh).astype(q.dtype)


def _rotate(t, cos, sin):
    t1, t2 = jnp.split(t, 2, axis=-1)
    c = cos[None, :, None, :]
    s = sin[None, :, None, :]
    return jnp.concatenate([t1 * c - t2 * s, t1 * s + t2 * c], axis=-1)


def retention(q, k, v, gn_g):
    B, S, H, Dk = q.shape
    Dv = v.shape[-1]
    C = RET_CHUNK
    nc = S // C
    gamma = 1.0 - 2.0 ** (-5.0 - np.arange(H))
    log_g = jnp.asarray(np.log(gamma).astype(np.float32))
    inv_freq = jnp.asarray((1.0 / (10000.0 ** np.linspace(0.0, 1.0, Dk // 2))).astype(np.float32))
    ang = jnp.arange(S, dtype=jnp.float32)[:, None] * inv_freq[None, :]
    cos, sin = jnp.cos(ang), jnp.sin(ang)
    qf = _rotate(q.astype(jnp.float32), cos, sin)
    kf = _rotate(k.astype(jnp.float32), cos, sin) * (Dk ** -0.5)
    vf = v.astype(jnp.float32)
    q_ch = qf.reshape(B, nc, C, H, Dk).transpose(1, 0, 3, 2, 4)
    k_ch = kf.reshape(B, nc, C, H, Dk).transpose(1, 0, 3, 2, 4)
    v_ch = vf.reshape(B, nc, C, H, Dv).transpose(1, 0, 3, 2, 4)
    idx = jnp.arange(C, dtype=jnp.float32)
    diff = idx[:, None] - idx[None, :]
    intra = jnp.where(diff >= 0, jnp.exp(log_g[:, None, None] * jnp.maximum(diff, 0.0)), 0.0)
    cross = jnp.exp(log_g[:, None] * (idx[None, :] + 1.0))[None, :, :, None]
    kdec = jnp.exp(log_g[:, None] * (C - 1.0 - idx[None, :]))[None, :, :, None]
    chunk_dec = jnp.exp(log_g * C)[None, :, None, None]

    def step(R, xs):
        qc, kc, vc = xs
        s = jnp.einsum('bhid,bhjd->bhij', qc, kc) * intra
        o = (jnp.einsum('bhij,bhjv->bhiv', s, vc)
             + jnp.einsum('bhid,bhdv->bhiv', qc, R) * cross)
        R = R * chunk_dec + jnp.einsum('bhjd,bhjv->bhdv', kc * kdec, vc)
        return R, o

    R0 = jnp.zeros((B, H, Dk, Dv), jnp.float32)
    _, o = lax.scan(step, R0, (q_ch, k_ch, v_ch))
    o = o.transpose(1, 0, 3, 2, 4).reshape(B, S, H, Dv)
    mu = jnp.mean(o, axis=-1, keepdims=True)
    var = jnp.mean(jnp.square(o - mu), axis=-1, keepdims=True)
    o = (o - mu) * lax.rsqrt(var + EPS)
    return (o.reshape(B, S, H * Dv) * gn_g.astype(jnp.float32)).astype(v.dtype)


def memory_attention(q, mem_k, mem_v):
    B, S, H, Dh = q.shape
    s = jnp.einsum('bshd,bmhd->bhsm', q, mem_k).astype(jnp.float32) * (Dh ** -0.5)
    p = jax.nn.softmax(s, axis=-1).astype(mem_v.dtype)
    return jnp.einsum('bhsm,bmhd->bshd', p, mem_v).reshape(B, S, H * Dh).astype(q.dtype)


def hybrid_layer(x, mem, pre_g, post_g, mem_g, w_in, w_mem_kv,
                 pe_k, w1_k, w2_k, pe_v, w1_v, w2_v, ret_gn_g, w_out):
    B, S, _ = x.shape
    h = rms_norm(x, pre_g)
    proj = h @ w_in
    (mq, mk, mv, nq, nkc, nvc, nks, nvs, nkw, nvw, ngate,
     rq, rk, rv, cq, z) = split_cols(proj, IN_SPLITS)

    def heads(t, n, d):
        return t.reshape(B, S, n, d)

    o_moba = moba_attention(heads(mq, MOBA_HEADS, MOBA_HEAD_DIM),
                            heads(mk, MOBA_HEADS, MOBA_HEAD_DIM),
                            heads(mv, MOBA_HEADS, MOBA_HEAD_DIM))
    kvh = lambda t: heads(t, NSA_KV_HEADS, NSA_HEAD_DIM)
    o_nsa = nsa_attention(heads(nq, NSA_HEADS, NSA_HEAD_DIM),
                          kvh(nkc), kvh(nvc), kvh(nks), kvh(nvs), kvh(nkw), kvh(nvw),
                          ngate.reshape(B, S, NSA_HEADS, 3),
                          pe_k, w1_k, w2_k, pe_v, w1_v, w2_v)
    o_ret = retention(heads(rq, RET_HEADS, RET_KEY_DIM), heads(rk, RET_HEADS, RET_KEY_DIM),
                      heads(rv, RET_HEADS, RET_VAL_DIM), ret_gn_g)
    mem_n = rms_norm(mem, mem_g)
    mem_k, mem_v = jnp.split(mem_n @ w_mem_kv, 2, axis=-1)
    M = mem.shape[1]
    o_mem = memory_attention(heads(cq, MEM_HEADS, MEM_HEAD_DIM),
                             mem_k.reshape(B, M, MEM_HEADS, MEM_HEAD_DIM),
                             mem_v.reshape(B, M, MEM_HEADS, MEM_HEAD_DIM))
    o = jnp.concatenate([o_moba, o_nsa, o_ret, o_mem], axis=-1) * jax.nn.silu(z)
    y = o @ w_out
    return x + rms_norm(y, post_g)


def setup_inputs(seed: int = 0) -> dict:
    key = jax.random.key(seed)
    ks = jax.random.split(key, 16)
    f32 = jnp.float32

    def normal(k, shape, scale):
        return jax.random.normal(k, shape, f32) * scale

    L = DEPTH
    cmp_in = NSA_CMP_LEN * NSA_HEAD_DIM
    return {
        "x": normal(ks[0], (BATCH, SEQ, D_MODEL), 1.0),
        "mem": normal(ks[1], (BATCH, MEM_LEN, D_MODEL), 1.0),
        "pre_norm_g": 1.0 + normal(ks[2], (L, D_MODEL), 0.02),
        "post_norm_g": 1.0 + normal(ks[3], (L, D_MODEL), 0.02),
        "mem_norm_g": 1.0 + normal(ks[4], (L, D_MODEL), 0.02),
        "w_in": normal(ks[5], (L, D_MODEL, IN_COLS), D_MODEL ** -0.5),
        "w_mem_kv": normal(ks[6], (L, D_MODEL, 2 * GROUP_WIDTH), D_MODEL ** -0.5),
        "nsa_pe_k": normal(ks[7], (L, NSA_CMP_LEN, NSA_HEAD_DIM), 0.1),
        "nsa_w1_k": normal(ks[8], (L, cmp_in, NSA_CMP_HIDDEN), cmp_in ** -0.5),
        "nsa_w2_k": normal(ks[9], (L, NSA_CMP_HIDDEN, NSA_HEAD_DIM), NSA_CMP_HIDDEN ** -0.5),
        "nsa_pe_v": normal(ks[10], (L, NSA_CMP_LEN, NSA_HEAD_DIM), 0.1),
        "nsa_w1_v": normal(ks[11], (L, cmp_in, NSA_CMP_HIDDEN), cmp_in ** -0.5),
        "nsa_w2_v": normal(ks[12], (L, NSA_CMP_HIDDEN, NSA_HEAD_DIM), NSA_CMP_HIDDEN ** -0.5),
        "ret_gn_g": 1.0 + normal(ks[13], (L, GROUP_WIDTH), 0.02),
        "w_out": normal(ks[14], (L, MIX_WIDTH, D_MODEL), MIX_WIDTH ** -0.5),
    }


def reference(x, mem, pre_norm_g, post_norm_g, mem_norm_g, w_in, w_mem_kv,
              nsa_pe_k, nsa_w1_k, nsa_w2_k, nsa_pe_v, nsa_w1_v, nsa_w2_v,
              ret_gn_g, w_out):
    for l in range(DEPTH):
        x = hybrid_layer(x, mem, pre_norm_g[l], post_norm_g[l], mem_norm_g[l],
                         w_in[l], w_mem_kv[l],
                         nsa_pe_k[l], nsa_w1_k[l], nsa_w2_k[l],
                         nsa_pe_v[l], nsa_w1_v[l], nsa_w2_v[l],
                         ret_gn_g[l], w_out[l])
    return x
```

```python
import functools

import numpy as np
import jax
import jax.numpy as jnp
from jax import lax
from jax.experimental import pallas as pl
from jax.experimental.pallas import tpu as pltpu

F32 = jnp.float32
BF16 = jnp.bfloat16

EPS = 1e-6
LANES = 128
HALF = 64
NEG = -1e30
NEG_BIAS = -float(2 ** 30)
VMEM_LIMIT = 56 * 1024 * 1024

MOBA_BLOCK = 256
MOBA_TOPK = 3
NSA_CMP_LEN = 32
NSA_CMP_STRIDE = 16
NSA_SLC_BLOCK = 64
NSA_SLC_TOPK = 16
NSA_WINDOW = 512
NSA_TQ = 128
NSA_KV_CHUNK = 256
RET_HEADS = 4
RET_KEY_DIM = 64
RET_VAL_DIM = 128
RET_TILE = 256
MEM_TQ = 512
ROW_TILE = 512


def _nt(a, b):
    return lax.dot_general(a, b, (((1,), (1,)), ((), ())), preferred_element_type=F32)


def _mm(a, b):
    return jnp.dot(a, b, preferred_element_type=F32)


def _iota(shape, dim):
    return lax.broadcasted_iota(jnp.int32, shape, dim)


def _swap_halves(x):
    return pltpu.roll(x.astype(F32), HALF, 1).astype(x.dtype)


def _params(*sem):
    return pltpu.CompilerParams(dimension_semantics=sem, vmem_limit_bytes=VMEM_LIMIT)


def _norm_matmul_kernel(x_ref, g_ref, w_ref, *out_refs, widths):
    x = x_ref[...]
    ms = jnp.mean(x * x, axis=-1, keepdims=True)
    h = (x * lax.rsqrt(ms + EPS) * g_ref[...]).astype(BF16)
    off = 0
    for o_ref, wd in zip(out_refs, widths):
        o_ref[...] = _mm(h, w_ref[:, off:off + wd]).astype(o_ref.dtype)
        off += wd


def _norm_matmul(x2, g, w, widths, dtypes, name):
    n, d = x2.shape
    tm = min(ROW_TILE, n)
    nc = w.shape[1]
    assert sum(widths) == nc and n % tm == 0
    return pl.pallas_call(
        functools.partial(_norm_matmul_kernel, widths=tuple(widths)),
        grid=(n // tm,),
        in_specs=[pl.BlockSpec((tm, d), lambda i: (i, 0)),
                  pl.BlockSpec((1, d), lambda i: (0, 0)),
                  pl.BlockSpec((d, nc), lambda i: (0, 0))],
        out_specs=[pl.BlockSpec((tm, wd), lambda i: (i, 0)) for wd in widths],
        out_shape=[jax.ShapeDtypeStruct((n, wd), dt) for wd, dt in zip(widths, dtypes)],
        compiler_params=_params("arbitrary"),
        name=name,
    )(x2, g.reshape(1, d), w)


def _compress_kernel(xk_ref, xv_ref, pek_ref, pev_ref, wak_ref, wbk_ref, w2k_ref,
                     wav_ref, wbv_ref, w2v_ref, kc_ref, vc_ref):
    def one(x_ref, pe_ref, wa_ref, wb_ref, w2_ref, o_ref):
        x = x_ref[0].astype(F32)
        pe = pe_ref[...]
        u = _mm((x + pe[0:1, :]).astype(BF16), wa_ref[...])
        v = _mm((x + pe[1:2, :]).astype(BF16), wb_ref[...])
        n = u.shape[0]
        hid = u + pltpu.roll(v, n - 1, 0)
        hid = hid * (1.0 / (1.0 + jnp.exp(-hid)))
        o_ref[0] = _mm(hid.astype(BF16), w2_ref[...]).astype(o_ref.dtype)

    one(xk_ref, pek_ref, wak_ref, wbk_ref, w2k_ref, kc_ref)
    one(xv_ref, pev_ref, wav_ref, wbv_ref, w2v_ref, vc_ref)


def _compress_weights(pe, w1, w2):
    l, dh = pe.shape
    hidden = w1.shape[1]
    half = l // 2
    eye = jnp.eye(2, dtype=F32)
    w1r = w1.reshape(2, half, dh, hidden)
    wab = jnp.einsum('sodj,gh->sogdhj', w1r, eye).reshape(2, half * 2 * dh, 2 * hidden)
    w2b = jnp.einsum('jd,gh->gjhd', w2, eye).reshape(2 * hidden, 2 * dh)
    pe2 = jnp.broadcast_to(pe.reshape(2, half, 1, dh), (2, half, 2, dh)).reshape(2, half * 2 * dh)
    return pe2, wab[0].astype(BF16), wab[1].astype(BF16), w2b.astype(BF16)


def _compress(nkc, nvc, pe_k, w1_k, w2_k, pe_v, w1_v, w2_v, b, s):
    rows = s // NSA_CMP_STRIDE
    width = NSA_CMP_STRIDE * LANES
    xk = nkc.reshape(b, rows, width)
    xv = nvc.reshape(b, rows, width)
    pk, wak, wbk, w2k = _compress_weights(pe_k, w1_k, w2_k)
    pv, wav, wbv, w2v = _compress_weights(pe_v, w1_v, w2_v)
    hid2 = wak.shape[1]
    xspec = pl.BlockSpec((1, rows, width), lambda i: (i, 0, 0))
    full = lambda a: pl.BlockSpec(a.shape, lambda i: (0,) * a.ndim)
    ospec = pl.BlockSpec((1, rows, LANES), lambda i: (i, 0, 0))
    return pl.pallas_call(
        _compress_kernel,
        grid=(b,),
        in_specs=[xspec, xspec, full(pk), full(pv), full(wak), full(wbk), full(w2k),
                  full(wav), full(wbv), full(w2v)],
        out_specs=[ospec, ospec],
        out_shape=[jax.ShapeDtypeStruct((b, rows, LANES), BF16)] * 2,
        compiler_params=_params("arbitrary"),
        name="nsa_compress",
    )(xk, xv, pk, pv, wak, wbk, w2k, wav, wbv, w2v)


def _moba_kernel(q_ref, k_ref, v_ref, oh_ref, o_ref, kmean_ref):
    blk = q_ref.shape[1]
    s_len = k_ref.shape[1]
    nb = s_len // blk
    qt = pl.program_id(2)

    @pl.when(qt == 0)
    def _():
        for j in range(nb):
            kj = k_ref[0, j * blk:(j + 1) * blk, :].astype(F32)
            kmean_ref[j:j + 1, :] = jnp.sum(kj, axis=0, keepdims=True) * (1.0 / blk)

    q = q_ref[0]
    lane = _iota((blk, LANES), 1)
    lo = lane < HALF
    zero = jnp.zeros_like(q)
    q2 = jnp.concatenate([jnp.where(lo, q, zero), jnp.where(lo, zero, q)], axis=0)

    gt = _nt(kmean_ref[...].astype(BF16), q2)
    jidx = _iota(gt.shape, 0)
    cnt = jnp.zeros(gt.shape, F32)
    for i in range(nb):
        gi = gt[i:i + 1, :]
        beats = (gi > gt) | ((gi == gt) & (jidx > i))
        past_i = jnp.full(gt.shape, i, jnp.int32) < qt
        cnt = cnt + jnp.where(beats & past_i, 1.0, 0.0)
    keep = ((jidx < qt) & (cnt < float(MOBA_TOPK))) | (jidx == qt)
    bias_t = jnp.where(keep, 0.0, NEG_BIAS)
    bias_t = jnp.concatenate([bias_t, jnp.zeros((LANES - nb, 2 * blk), F32)], axis=0)
    bias_q = bias_t.T.astype(BF16)
    q_aug = jnp.concatenate([q2 * jnp.asarray(HALF ** -0.5, BF16), bias_q], axis=1)

    def scores(j):
        st = pl.multiple_of(j * blk, blk)
        k_aug = jnp.concatenate([k_ref[0, pl.ds(st, blk), :], oh_ref[pl.ds(st, blk), :]], axis=1)
        return _nt(q_aug, k_aug), v_ref[0, pl.ds(st, blk), :]

    s, vj = scores(qt)
    rq = _iota(s.shape, 0) & (blk - 1)
    ck = _iota(s.shape, 1)
    s = jnp.where(ck <= rq, s, NEG)
    m = jnp.max(s, axis=1, keepdims=True)
    p = jnp.exp(s - m)
    l = jnp.sum(p, axis=1, keepdims=True)
    acc = _mm(p.astype(BF16), vj)

    def body(j, carry):
        m, l, acc = carry
        s, vj = scores(j)
        m_new = jnp.maximum(m, jnp.max(s, axis=1, keepdims=True))
        alpha = jnp.exp(m - m_new)
        p = jnp.exp(s - m_new)
        l = alpha * l + jnp.sum(p, axis=1, keepdims=True)
        acc = alpha * acc + _mm(p.astype(BF16), vj)
        return m_new, l, acc

    m, l, acc = lax.fori_loop(0, qt, body, (m, l, acc))
    out = acc / l
    o_ref[0] = jnp.where(lo, out[:blk], out[blk:]).astype(o_ref.dtype)


def _moba(mq, mk, mv, b, s):
    width = mq.shape[-1]
    blk = MOBA_BLOCK
    nb = s // blk
    assert nb <= LANES
    oh = (jnp.arange(s)[:, None] // blk == jnp.arange(LANES)[None, :]).astype(BF16)
    qspec = pl.BlockSpec((1, blk, LANES), lambda bi, hp, qt: (bi, qt, hp))
    kspec = pl.BlockSpec((1, s, LANES), lambda bi, hp, qt: (bi, 0, hp))
    return pl.pallas_call(
        _moba_kernel,
        grid=(b, width // LANES, nb),
        in_specs=[qspec, kspec, kspec, pl.BlockSpec((s, LANES), lambda bi, hp, qt: (0, 0))],
        out_specs=qspec,
        out_shape=jax.ShapeDtypeStruct((b, s, width), BF16),
        scratch_shapes=[pltpu.VMEM((nb, LANES), F32)],
        compiler_params=_params("arbitrary", "arbitrary", "arbitrary"),
        name="moba_attention",
    )(mq.reshape(b, s, width), mk.reshape(b, s, width), mv.reshape(b, s, width), oh)


def _masked_softmax(s, mask):
    s = jnp.where(mask, s, NEG)
    m = jnp.max(s, axis=1, keepdims=True)
    m = jnp.where(m <= 0.5 * NEG, 0.0, m)
    p = jnp.exp(s - m)
    den = jnp.sum(p, axis=1, keepdims=True)
    return p / jnp.where(den > 0, den, 1.0)


def _nsa_kernel(q_ref, gate_ref, kc_ref, vc_ref, slw_ref, oh_ref, c2s_ref, o_ref):
    tq = q_ref.shape[1]
    s_len = slw_ref.shape[1]
    n_cmp = kc_ref.shape[1]
    rows = 4 * tq
    t0 = pl.program_id(1) * tq
    scale = jnp.asarray(HALF ** -0.5, BF16)

    lane = _iota((tq, LANES), 1)
    lo = lane < HALF
    qb = [q_ref[0, :, LANES * c:LANES * (c + 1)] * scale for c in range(4)]
    qb_sw = [_swap_halves(x) for x in qb]
    zero = jnp.zeros((tq, LANES), BF16)

    def group_queries(g):
        in_g = lo if g == 0 else jnp.logical_not(lo)
        parts = []
        for p in range(4):
            h = 4 * g + p
            x = qb[h // 2] if (h % 2) == g else qb_sw[h // 2]
            parts.append(jnp.where(in_g, x, zero))
        return jnp.concatenate(parts, axis=0), in_g

    qg = [group_queries(g) for g in range(2)]
    tqv_row = t0 + (_iota((rows, 1), 0) & (tq - 1))

    kc = kc_ref[0]
    vc = vc_ref[0]
    cmp_end = NSA_CMP_STRIDE * _iota((rows, n_cmp), 1) + (NSA_CMP_LEN - 1)
    cmask = cmp_end <= tqv_row
    o_cmp = []
    imp = jnp.zeros((tq, LANES), F32)
    for g in range(2):
        p = _masked_softmax(_nt(qg[g][0], kc), cmask).astype(BF16)
        o_cmp.append(_mm(p, vc))
        i4 = _mm(p, c2s_ref[g])
        imp = imp + (i4[0:tq] + i4[tq:2 * tq]) + (i4[2 * tq:3 * tq] + i4[3 * tq:4 * tq])

    imp_t = imp.T
    nblk = NSA_SLC_BLOCK
    jrow = _iota((LANES, tq), 0) & (nblk - 1)
    own = (t0 + _iota((LANES, tq), 1)) >> 6
    forced = (jrow == 0) | (jrow == own) | (jrow == own - 1)
    valid = jrow <= own
    val = jnp.where(valid, jnp.where(forced, jnp.inf, imp_t), -jnp.inf)
    sub = 8
    jloc = _iota((sub, tq), 0)
    cnts = []
    for g in range(2):
        vg = val[nblk * g:nblk * (g + 1)]
        tiles = [vg[sub * r:sub * (r + 1)] for r in range(nblk // sub)]
        cnt_r = [jnp.zeros((sub, tq), F32) for _ in tiles]
        for i in range(nblk):
            vi = jnp.broadcast_to(vg[i:i + 1, :], (sub, tq))
            for r, vr in enumerate(tiles):
                if i < sub * r:
                    beats = vi >= vr
                elif i >= sub * (r + 1):
                    beats = vi > vr
                else:
                    beats = (vi > vr) | ((vi == vr) & (jloc > i - sub * r))
                cnt_r[r] = cnt_r[r] + jnp.where(beats, 1.0, 0.0)
        cnts.extend(cnt_r)
    cnt = jnp.concatenate(cnts, axis=0)
    keep = valid & (cnt < float(NSA_SLC_TOPK))
    bias_q = jnp.where(keep, 0.0, NEG_BIAS).T

    gate = 1.0 / (1.0 + jnp.exp(-gate_ref[0]))
    ck = NSA_KV_CHUNK
    jd = pl.program_id(1) // (ck // tq)
    w0 = pl.multiple_of(jnp.maximum(t0 - NSA_WINDOW, 0), tq)
    wspan = NSA_WINDOW + tq
    heads = [None] * 8
    for g in range(2):
        q4, in_g = qg[g]
        bias_g = jnp.where(in_g, bias_q, 0.0).astype(BF16)
        q_aug = jnp.concatenate([q4, jnp.concatenate([bias_g] * 4, axis=0)], axis=1)

        def chunk(j):
            st = pl.multiple_of(j * ck, ck)
            k_aug = jnp.concatenate([slw_ref[0, pl.ds(st, ck), 0:LANES], oh_ref[pl.ds(st, ck), :]],
                                    axis=1)
            return _nt(q_aug, k_aug), slw_ref[0, pl.ds(st, ck), LANES:2 * LANES]

        s, vj = chunk(jd)
        kpos = jd * ck + _iota(s.shape, 1)
        s = jnp.where(kpos <= tqv_row, s, NEG)
        m = jnp.max(s, axis=1, keepdims=True)
        p = jnp.exp(s - m)
        l = jnp.sum(p, axis=1, keepdims=True)
        acc = _mm(p.astype(BF16), vj)

        def body(j, carry):
            m, l, acc = carry
            s, vj = chunk(j)
            m_new = jnp.maximum(m, jnp.max(s, axis=1, keepdims=True))
            alpha = jnp.exp(m - m_new)
            p = jnp.exp(s - m_new)
            l = alpha * l + jnp.sum(p, axis=1, keepdims=True)
            acc = alpha * acc + _mm(p.astype(BF16), vj)
            return m_new, l, acc

        m, l, acc = lax.fori_loop(0, jd, body, (m, l, acc))
        o_slc = acc / l

        kw = slw_ref[0, pl.ds(w0, wspan), 2 * LANES:3 * LANES]
        vw = slw_ref[0, pl.ds(w0, wspan), 3 * LANES:4 * LANES]
        kp = w0 + _iota((rows, wspan), 1)
        wmask = (kp <= tqv_row) & (kp > tqv_row - NSA_WINDOW)
        pw = _masked_softmax(_nt(q4, kw), wmask).astype(BF16)
        o_win = _mm(pw, vw)

        for p_i in range(4):
            h = 4 * g + p_i
            r = slice(p_i * tq, (p_i + 1) * tq)
            heads[h] = (gate[:, 3 * h:3 * h + 1] * o_cmp[g][r]
                        + gate[:, 3 * h + 1:3 * h + 2] * o_slc[r]
                        + gate[:, 3 * h + 2:3 * h + 3] * o_win[r])

    for c in range(4):
        g = c // 2
        a_lo, a_hi = heads[2 * c], heads[2 * c + 1]
        if g == 0:
            a_hi = pltpu.roll(a_hi, HALF, 1)
        else:
            a_lo = pltpu.roll(a_lo, HALF, 1)
        o_ref[0, :, LANES * c:LANES * (c + 1)] = jnp.where(lo, a_lo, a_hi).astype(o_ref.dtype)


def _cmp_to_slc(n_cmp_pad, n_cmp, n_slc):
    rs = NSA_SLC_BLOCK // NSA_CMP_STRIDE
    rc = NSA_CMP_LEN // NSA_CMP_STRIDE
    j = np.arange(n_slc)[:, None, None]
    i = np.broadcast_to(rs * j + np.arange(rs)[None, :, None] - np.arange(rc)[None, None, :],
                        (n_slc, rs, rc))
    jj = np.broadcast_to(j, i.shape)
    ok = (i >= 0) & (i < n_cmp)
    mat = np.zeros((n_cmp, n_slc), np.float32)
    np.add.at(mat, (i[ok], jj[ok]), 1.0)
    out = np.zeros((2, n_cmp_pad, LANES), np.float32)
    for g in range(2):
        out[g, :n_cmp, HALF * g:HALF * g + n_slc] = mat
    return jnp.asarray(out, BF16)


def _nsa(nq, ngate, kc, vc, nslw, b, s):
    tq = NSA_TQ
    n_slc = s // NSA_SLC_BLOCK
    n_cmp = (s - NSA_CMP_LEN) // NSA_CMP_STRIDE + 1
    n_cmp_pad = kc.shape[1]
    assert n_slc <= HALF and s >= NSA_WINDOW + tq and s % NSA_KV_CHUNK == 0
    width = nq.shape[-1]
    oh = ((jnp.arange(s)[:, None] // NSA_SLC_BLOCK)
          == (jnp.arange(LANES)[None, :] % HALF)).astype(BF16)
    c2s = _cmp_to_slc(n_cmp_pad, n_cmp, n_slc)
    tile = lambda w: pl.BlockSpec((1, tq, w), lambda bi, qt: (bi, qt, 0))
    perb = lambda r, w: pl.BlockSpec((1, r, w), lambda bi, qt: (bi, 0, 0))
    return pl.pallas_call(
        _nsa_kernel,
        grid=(b, s // tq),
        in_specs=[tile(width), tile(LANES), perb(n_cmp_pad, LANES), perb(n_cmp_pad, LANES),
                  perb(s, 4 * LANES), pl.BlockSpec((s, LANES), lambda bi, qt: (0, 0)),
                  pl.BlockSpec(c2s.shape, lambda bi, qt: (0, 0, 0))],
        out_specs=tile(width),
        out_shape=jax.ShapeDtypeStruct((b, s, width), BF16),
        compiler_params=_params("arbitrary", "arbitrary"),
        name="nsa_attention",
    )(nq.reshape(b, s, width), ngate.reshape(b, s, LANES), kc, vc,
      nslw.reshape(b, s, 4 * LANES), oh, c2s)


def _ret_kernel(qk_ref, v_ref, cos_ref, sin_ref, dec_ref, rowdec_ref, cdec_ref, gn_ref, o_ref, r_ref):
    ct = qk_ref.shape[1]
    kw = RET_HEADS * RET_KEY_DIM

    @pl.when(pl.program_id(1) == 0)
    def _():
        r_ref[...] = jnp.zeros_like(r_ref)

    cos = cos_ref[...]
    sin = sin_ref[...]
    lane = _iota((ct, LANES), 1)
    lo = lane < HALF
    first = (lane & (RET_KEY_DIM - 1)) < RET_KEY_DIM // 2

    def rotate(t):
        partner = jnp.where(first, pltpu.roll(t, LANES - RET_KEY_DIM // 2, 1),
                            pltpu.roll(t, RET_KEY_DIM // 2, 1))
        return t * cos + partner * sin

    rowdec = rowdec_ref[...]
    for c in range(RET_HEADS // 2):
        qc = rotate(qk_ref[0, :, LANES * c:LANES * (c + 1)])
        kc = rotate(qk_ref[0, :, kw + LANES * c:kw + LANES * (c + 1)]) * (RET_KEY_DIM ** -0.5)
        kcb = kc.astype(BF16)
        r_old = r_ref[c]
        r_new = r_old * cdec_ref[c]
        r_oldb = r_old.astype(BF16)
        for half in range(2):
            h = 2 * c + half
            in_h = lo if half == 0 else jnp.logical_not(lo)
            qh = jnp.where(in_h, qc, 0.0).astype(BF16)
            vh = v_ref[0, :, RET_VAL_DIM * h:RET_VAL_DIM * (h + 1)]
            sc = (_nt(qh, kcb) * dec_ref[h]).astype(BF16)
            o = _mm(sc, vh) + _mm(qh, r_oldb) * rowdec[:, h:h + 1]
            kd = jnp.where(in_h, kc, 0.0) * rowdec[:, 4 + h:5 + h]
            r_new = r_new + _mm(kd.T.astype(BF16), vh)
            mu = jnp.mean(o, axis=-1, keepdims=True)
            d = o - mu
            var = jnp.mean(d * d, axis=-1, keepdims=True)
            o = d * lax.rsqrt(var + EPS) * gn_ref[:, RET_VAL_DIM * h:RET_VAL_DIM * (h + 1)]
            o_ref[0, :, RET_VAL_DIM * h:RET_VAL_DIM * (h + 1)] = o.astype(o_ref.dtype)
        r_ref[c] = r_new


def _ret_tables(s, ct):
    h, dk = RET_HEADS, RET_KEY_DIM
    gamma = 1.0 - 2.0 ** (-5.0 - np.arange(h))
    log_g = jnp.asarray(np.log(gamma).astype(np.float32))
    inv_freq = jnp.asarray((1.0 / (10000.0 ** np.linspace(0.0, 1.0, dk // 2))).astype(np.float32))
    ang = jnp.arange(s, dtype=F32)[:, None] * inv_freq[None, :]
    cos, sin = jnp.cos(ang), jnp.sin(ang)
    cos_t = jnp.tile(jnp.concatenate([cos, cos], axis=-1), (1, LANES // dk))
    sin_t = jnp.tile(jnp.concatenate([-sin, sin], axis=-1), (1, LANES // dk))
    idx = jnp.arange(ct, dtype=F32)
    diff = idx[:, None] - idx[None, :]
    intra = jnp.where(diff >= 0, jnp.exp(log_g[:, None, None] * jnp.maximum(diff, 0.0)), 0.0)
    cross = jnp.exp(log_g[:, None] * (idx[None, :] + 1.0))
    kdec = jnp.exp(log_g[:, None] * (ct - 1.0 - idx[None, :]))
    rowdec = jnp.zeros((ct, LANES), F32).at[:, 0:h].set(cross.T).at[:, h:2 * h].set(kdec.T)
    cd = jnp.exp(log_g * ct)
    cdec = jnp.broadcast_to(jnp.repeat(cd, HALF).reshape(h // 2, LANES, 1), (h // 2, LANES, LANES))
    return cos_t, sin_t, intra, rowdec, cdec


def _retention(rqk, rv, gn_g, b, s):
    ct = min(RET_TILE, s)
    cos_t, sin_t, intra, rowdec, cdec = _ret_tables(s, ct)
    wqk = rqk.shape[-1]
    wv = rv.shape[-1]
    kw = wqk // 2
    const = lambda a: pl.BlockSpec(a.shape, lambda bi, ci: (0,) * a.ndim)
    return pl.pallas_call(
        _ret_kernel,
        grid=(b, s // ct),
        in_specs=[pl.BlockSpec((1, ct, wqk), lambda bi, ci: (bi, ci, 0)),
                  pl.BlockSpec((1, ct, wv), lambda bi, ci: (bi, ci, 0)),
                  pl.BlockSpec((ct, LANES), lambda bi, ci: (ci, 0)),
                  pl.BlockSpec((ct, LANES), lambda bi, ci: (ci, 0)),
                  const(intra), const(rowdec), const(cdec),
                  pl.BlockSpec((1, wv), lambda bi, ci: (0, 0))],
        out_specs=pl.BlockSpec((1, ct, wv), lambda bi, ci: (bi, ci, 0)),
        out_shape=jax.ShapeDtypeStruct((b, s, wv), BF16),
        scratch_shapes=[pltpu.VMEM((RET_HEADS // 2, LANES, LANES), F32)],
        compiler_params=_params("arbitrary", "arbitrary"),
        name="retention",
    )(rqk.reshape(b, s, wqk), rv.reshape(b, s, wv), cos_t, sin_t, intra, rowdec, cdec,
      gn_g.reshape(1, wv))


def _mem_kernel(q_ref, k_ref, v_ref, o_ref):
    nh = q_ref.shape[2] // LANES
    scale = jnp.asarray(LANES ** -0.5, F32)
    for h in range(nh):
        cs = slice(LANES * h, LANES * (h + 1))
        s = _nt(q_ref[0, :, cs], k_ref[0, :, cs]) * scale
        m = jnp.max(s, axis=1, keepdims=True)
        p = jnp.exp(s - m)
        p = (p / jnp.sum(p, axis=1, keepdims=True)).astype(BF16)
        o_ref[0, :, cs] = _mm(p, v_ref[0, :, cs]).astype(o_ref.dtype)


def _mem_attention(cq, mem_k, mem_v, b, s):
    width = cq.shape[-1]
    m = mem_k.shape[0] // b
    tq = min(MEM_TQ, s)
    return pl.pallas_call(
        _mem_kernel,
        grid=(b, s // tq),
        in_specs=[pl.BlockSpec((1, tq, width), lambda bi, qt: (bi, qt, 0)),
                  pl.BlockSpec((1, m, width), lambda bi, qt: (bi, 0, 0)),
                  pl.BlockSpec((1, m, width), lambda bi, qt: (bi, 0, 0))],
        out_specs=pl.BlockSpec((1, tq, width), lambda bi, qt: (bi, qt, 0)),
        out_shape=jax.ShapeDtypeStruct((b, s, width), BF16),
        compiler_params=_params("arbitrary", "arbitrary"),
        name="memory_attention",
    )(cq.reshape(b, s, width), mem_k.reshape(b, m, width), mem_v.reshape(b, m, width))


def _out_kernel(oa_ref, ob_ref, oc_ref, od_ref, z_ref, x_ref, w_ref, g_ref, y_ref):
    gw = oa_ref.shape[1]
    acc = None
    for i, o_ref in enumerate((oa_ref, ob_ref, oc_ref, od_ref)):
        z = z_ref[:, gw * i:gw * (i + 1)].astype(F32)
        gated = (o_ref[...].astype(F32) * (z * (1.0 / (1.0 + jnp.exp(-z))))).astype(BF16)
        part = _mm(gated, w_ref[gw * i:gw * (i + 1), :])
        acc = part if acc is None else acc + part
    ms = jnp.mean(acc * acc, axis=-1, keepdims=True)
    y_ref[...] = x_ref[...] + acc * lax.rsqrt(ms + EPS) * g_ref[...]


def _out_proj(oa, ob, oc, od, z, x2, w_out, post_g):
    n, d = x2.shape
    gw = oa.shape[-1]
    tm = min(ROW_TILE, n)
    rows = lambda w: pl.BlockSpec((tm, w), lambda i: (i, 0))
    return pl.pallas_call(
        _out_kernel,
        grid=(n // tm,),
        in_specs=[rows(gw), rows(gw), rows(gw), rows(gw), rows(4 * gw), rows(d),
                  pl.BlockSpec(w_out.shape, lambda i: (0, 0)),
                  pl.BlockSpec((1, d), lambda i: (0, 0))],
        out_specs=rows(d),
        out_shape=jax.ShapeDtypeStruct((n, d), F32),
        compiler_params=_params("arbitrary"),
        name="gate_out_proj",
    )(oa.reshape(n, gw), ob.reshape(n, gw), oc.reshape(n, gw), od.reshape(n, gw), z, x2,
      w_out, post_g.reshape(1, d))


def _pad_gate_cols(w_in, gate_off, gate_w):
    d = w_in.shape[0]
    return jnp.concatenate([w_in[:, :gate_off + gate_w],
                            jnp.zeros((d, LANES - gate_w), w_in.dtype),
                            w_in[:, gate_off + gate_w:]], axis=1)


def _layer(x2, mem2, b, s, pre_g, post_g, mem_g, w_in, w_mem_kv,
           pe_k, w1_k, w2_k, pe_v, w1_v, w2_v, ret_gn_g, w_out):
    gw = w_out.shape[0] // 4
    kvw = LANES
    gate_w = 3 * 8
    gate_off = 4 * gw + 6 * kvw
    w = _pad_gate_cols(w_in, gate_off, gate_w).astype(BF16)
    widths = (gw, gw, gw, gw, kvw, kvw, 4 * kvw, LANES, gw, gw, gw, 4 * gw)
    dtypes = (BF16, BF16, BF16, BF16, BF16, BF16, BF16, F32, F32, BF16, BF16, BF16)
    (mq, mk, mv, nq, nkc, nvc, nslw, ngate, rqk, rv, cq, z) = _norm_matmul(
        x2, pre_g, w, widths, dtypes, "norm_in_proj")
    mem_k, mem_v = _norm_matmul(mem2, mem_g, w_mem_kv.astype(BF16), (gw, gw), (BF16, BF16),
                                "norm_mem_kv")

    o_moba = _moba(mq, mk, mv, b, s)
    kc, vc = _compress(nkc, nvc, pe_k, w1_k, w2_k, pe_v, w1_v, w2_v, b, s)
    o_nsa = _nsa(nq, ngate, kc, vc, nslw, b, s)
    o_ret = _retention(rqk, rv, ret_gn_g, b, s)
    o_mem = _mem_attention(cq, mem_k, mem_v, b, s)
    return _out_proj(o_moba, o_nsa, o_ret, o_mem, z, x2, w_out.astype(BF16), post_g)


def kernel(x, mem, pre_norm_g, post_norm_g, mem_norm_g, w_in, w_mem_kv, nsa_pe_k, nsa_w1_k, nsa_w2_k,
           nsa_pe_v, nsa_w1_v, nsa_w2_v, ret_gn_g, w_out):
    b, s, d = x.shape
    x2 = x.reshape(b * s, d)
    mem2 = mem.reshape(b * mem.shape[1], d)
    for l in range(w_in.shape[0]):
        x2 = _layer(x2, mem2, b, s, pre_norm_g[l], post_norm_g[l], mem_norm_g[l], w_in[l], w_mem_kv[l],
                    nsa_pe_k[l], nsa_w1_k[l], nsa_w2_k[l], nsa_pe_v[l], nsa_w1_v[l], nsa_w2_v[l],
                    ret_gn_g[l], w_out[l])
    return x2.reshape(b, s, d)
```

```python
import functools

import numpy as np
import jax
import jax.numpy as jnp
from jax import lax
from jax.experimental import pallas as pl
from jax.experimental.pallas import tpu as pltpu

F32 = jnp.float32
BF16 = jnp.bfloat16

EPS = 1e-6
LANES = 128
HALF = 64
NEG = -1e30
NEG_BIAS = -float(2 ** 30)
VMEM_LIMIT = 56 * 1024 * 1024

FLASH_ROWS = 256
MOBA_BLOCK = 256
MOBA_TOPK = 3
MOBA_GROUP = 4
MOBA_PAIRS = 4
NSA_CMP_LEN = 32
NSA_CMP_STRIDE = 16
NSA_SLC_BLOCK = 64
NSA_SLC_TOPK = 16
NSA_WINDOW = 512
NSA_TQ = 256
NSA_SPAN = 1024
RET_HEADS = 4
RET_KEY_DIM = 64
RET_VAL_DIM = 128
RET_TILE = 256
MEM_TQ = 512
ROW_TILE = 512


def _nt(a, b):
    return lax.dot_general(a, b, (((1,), (1,)), ((), ())), preferred_element_type=F32)


def _mm(a, b):
    return jnp.dot(a, b, preferred_element_type=F32)


def _iota(shape, dim):
    return lax.broadcasted_iota(jnp.int32, shape, dim)


def _swap_halves(x):
    return pltpu.roll(x.astype(F32), HALF, 1).astype(x.dtype)


def _params(*sem):
    return pltpu.CompilerParams(dimension_semantics=sem, vmem_limit_bytes=VMEM_LIMIT)


def _norm_matmul_kernel(x_ref, g_ref, w_ref, *out_refs, widths):
    x = x_ref[...]
    ms = jnp.mean(x * x, axis=-1, keepdims=True)
    h = (x * lax.rsqrt(ms + EPS) * g_ref[...]).astype(BF16)
    off = 0
    for o_ref, wd in zip(out_refs, widths):
        o_ref[...] = _mm(h, w_ref[:, off:off + wd]).astype(o_ref.dtype)
        off += wd


def _norm_matmul(x2, g, w, widths, dtypes, name):
    n, d = x2.shape
    tm = min(ROW_TILE, n)
    nc = w.shape[1]
    assert sum(widths) == nc and n % tm == 0
    return pl.pallas_call(
        functools.partial(_norm_matmul_kernel, widths=tuple(widths)),
        grid=(n // tm,),
        in_specs=[pl.BlockSpec((tm, d), lambda i: (i, 0)),
                  pl.BlockSpec((1, d), lambda i: (0, 0)),
                  pl.BlockSpec((d, nc), lambda i: (0, 0))],
        out_specs=[pl.BlockSpec((tm, wd), lambda i: (i, 0)) for wd in widths],
        out_shape=[jax.ShapeDtypeStruct((n, wd), dt) for wd, dt in zip(widths, dtypes)],
        compiler_params=_params("arbitrary"),
        name=name,
    )(x2, g.reshape(1, d), w)


def _compress_kernel(xk_ref, xv_ref, pek_ref, pev_ref, wak_ref, wbk_ref, w2k_ref,
                     wav_ref, wbv_ref, w2v_ref, kc_ref, vc_ref):
    def one(x_ref, pe_ref, wa_ref, wb_ref, w2_ref, o_ref):
        x = x_ref[0].astype(F32)
        pe = pe_ref[...]
        u = _mm((x + pe[0:1, :]).astype(BF16), wa_ref[...])
        v = _mm((x + pe[1:2, :]).astype(BF16), wb_ref[...])
        n = u.shape[0]
        hid = u + pltpu.roll(v, n - 1, 0)
        hid = hid * (1.0 / (1.0 + jnp.exp(-hid)))
        o_ref[0] = _mm(hid.astype(BF16), w2_ref[...]).astype(o_ref.dtype)

    one(xk_ref, pek_ref, wak_ref, wbk_ref, w2k_ref, kc_ref)
    one(xv_ref, pev_ref, wav_ref, wbv_ref, w2v_ref, vc_ref)


def _compress_weights(pe, w1, w2):
    l, dh = pe.shape
    hidden = w1.shape[1]
    half = l // 2
    eye = jnp.eye(2, dtype=F32)
    w1r = w1.reshape(2, half, dh, hidden)
    wab = jnp.einsum('sodj,gh->sogdhj', w1r, eye).reshape(2, half * 2 * dh, 2 * hidden)
    w2b = jnp.einsum('jd,gh->gjhd', w2, eye).reshape(2 * hidden, 2 * dh)
    pe2 = jnp.broadcast_to(pe.reshape(2, half, 1, dh), (2, half, 2, dh)).reshape(2, half * 2 * dh)
    return pe2, wab[0].astype(BF16), wab[1].astype(BF16), w2b.astype(BF16)


def _compress(nkc, nvc, pe_k, w1_k, w2_k, pe_v, w1_v, w2_v, b, s):
    rows = s // NSA_CMP_STRIDE
    width = NSA_CMP_STRIDE * LANES
    xk = nkc.reshape(b, rows, width)
    xv = nvc.reshape(b, rows, width)
    pk, wak, wbk, w2k = _compress_weights(pe_k, w1_k, w2_k)
    pv, wav, wbv, w2v = _compress_weights(pe_v, w1_v, w2_v)
    hid2 = wak.shape[1]
    xspec = pl.BlockSpec((1, rows, width), lambda i: (i, 0, 0))
    full = lambda a: pl.BlockSpec(a.shape, lambda i: (0,) * a.ndim)
    ospec = pl.BlockSpec((1, rows, LANES), lambda i: (i, 0, 0))
    return pl.pallas_call(
        _compress_kernel,
        grid=(b,),
        in_specs=[xspec, xspec, full(pk), full(pv), full(wak), full(wbk), full(w2k),
                  full(wav), full(wbv), full(w2v)],
        out_specs=[ospec, ospec],
        out_shape=[jax.ShapeDtypeStruct((b, rows, LANES), BF16)] * 2,
        compiler_params=_params("arbitrary"),
        name="nsa_compress",
    )(xk, xv, pk, pv, wak, wbk, w2k, wav, wbv, w2v)


def _flash_step(q_aug, k_aug, v_aug, carry=None, bias=None):
    out = []
    for i in range(q_aug.shape[0] // FLASH_ROWS):
        s = _nt(q_aug[i * FLASH_ROWS:(i + 1) * FLASH_ROWS], k_aug)
        if bias is not None:
            nb = bias.shape[1]
            s = s + bias if nb == s.shape[1] else jnp.concatenate([s[:, :nb] + bias, s[:, nb:]], axis=1)
        m = jnp.max(s, axis=1, keepdims=True)
        if carry is None:
            acc = _mm(jnp.exp(s - m).astype(BF16), v_aug)
        else:
            m_old = carry[2 * i]
            m = jnp.maximum(m_old, m)
            acc = jnp.exp(m_old - m) * carry[2 * i + 1] + _mm(jnp.exp(s - m).astype(BF16), v_aug)
        out.extend((m, acc))
    return tuple(out)


def _flash_finish(carry):
    return jnp.concatenate([acc[:, :LANES] / acc[:, LANES:] for acc in carry[1::2]], axis=0)


def _moba_kernel(q_ref, k_ref, v_ref, oh_ref, o_ref, kmean_ref):
    blk = q_ref.shape[1]
    npair = q_ref.shape[2] // LANES
    nb = k_ref.shape[1] // blk
    grp = MOBA_GROUP
    qt = pl.program_id(2)

    @pl.when(qt == 0)
    def _():
        for c in range(npair):
            for j in range(nb):
                kj = k_ref[0, j * blk:(j + 1) * blk, LANES * c:LANES * (c + 1)].astype(F32)
                kmean_ref[c, j:j + 1, :] = jnp.sum(kj, axis=0, keepdims=True) * (1.0 / blk)

    lane = _iota((blk, LANES), 1)
    lo = lane < HALF
    causal = jnp.where(_iota((blk, blk), 1) <= _iota((blk, blk), 0), 0.0, NEG)
    ones = jnp.ones((grp * blk, LANES), BF16)
    zero = jnp.zeros((blk, LANES), BF16)

    q_aug = []
    for c in range(npair):
        q = q_ref[0, :, LANES * c:LANES * (c + 1)]
        q2 = jnp.concatenate([jnp.where(lo, q, zero), jnp.where(lo, zero, q)], axis=0)
        km = kmean_ref[c]
        km_hi = km.astype(BF16)
        gt = _nt(km_hi, q2) + _nt((km - km_hi.astype(F32)).astype(BF16), q2)
        jidx = _iota(gt.shape, 0)
        cnt = jnp.zeros(gt.shape, F32)
        for i in range(nb):
            gi = gt[i:i + 1, :]
            beats = (gi > gt) | ((gi == gt) & (jidx > i))
            past_i = jnp.full(gt.shape, i, jnp.int32) < qt
            cnt = cnt + jnp.where(beats & past_i, 1.0, 0.0)
        keep = ((jidx < qt) & (cnt < float(MOBA_TOPK))) | (jidx == qt)
        bias_t = jnp.concatenate([jnp.where(keep, 0.0, NEG_BIAS),
                                  jnp.zeros((LANES - nb - 1, 2 * blk), F32),
                                  jnp.full((1, 2 * blk), NEG_BIAS, F32)], axis=0)
        q_aug.append(jnp.concatenate([q2 * jnp.asarray(HALF ** -0.5, BF16),
                                      bias_t.T.astype(BF16)], axis=1))

    def group_kv(c, gi):
        ks, vs = [], []
        for t in range(grp):
            j = qt - (gi * grp + t)
            sk = pl.multiple_of(jnp.maximum(j, 0) * blk, blk)
            so = pl.multiple_of(jnp.where(j >= 0, j, nb) * blk, blk)
            ks.append(jnp.concatenate([k_ref[0, pl.ds(sk, blk), LANES * c:LANES * (c + 1)],
                                       oh_ref[pl.ds(so, blk), :]], axis=1))
            vs.append(v_ref[0, pl.ds(sk, blk), LANES * c:LANES * (c + 1)])
        return jnp.concatenate(ks, axis=0), jnp.concatenate([jnp.concatenate(vs, axis=0), ones], axis=1)

    nch = 2 * (2 * blk // FLASH_ROWS)
    carry = ()
    for c in range(npair):
        carry += _flash_step(q_aug[c], *group_kv(c, 0), bias=causal)

    def body(gi, carry):
        out = ()
        for c in range(npair):
            out += _flash_step(q_aug[c], *group_kv(c, gi), carry=carry[nch * c:nch * (c + 1)])
        return out

    carry = lax.fori_loop(1, qt // grp + 1, body, carry)
    for c in range(npair):
        out = _flash_finish(carry[nch * c:nch * (c + 1)])
        o_ref[0, :, LANES * c:LANES * (c + 1)] = jnp.where(lo, out[:blk], out[blk:]).astype(o_ref.dtype)


def _moba(mq, mk, mv, b, s):
    width = mq.shape[-1]
    blk = MOBA_BLOCK
    nb = s // blk
    pw = LANES * MOBA_PAIRS
    assert nb < LANES - 1 and nb % MOBA_GROUP == 0 and width % pw == 0
    blk_id = jnp.where(jnp.arange(s + blk) < s, jnp.arange(s + blk) // blk, LANES - 1)
    oh = (blk_id[:, None] == jnp.arange(LANES)[None, :]).astype(BF16)
    qspec = pl.BlockSpec((1, blk, pw), lambda bi, hp, qt: (bi, qt, hp))
    kspec = pl.BlockSpec((1, s, pw), lambda bi, hp, qt: (bi, 0, hp))
    return pl.pallas_call(
        _moba_kernel,
        grid=(b, width // pw, nb),
        in_specs=[qspec, kspec, kspec, pl.BlockSpec((s + blk, LANES), lambda bi, hp, qt: (0, 0))],
        out_specs=qspec,
        out_shape=jax.ShapeDtypeStruct((b, s, width), BF16),
        scratch_shapes=[pltpu.VMEM((MOBA_PAIRS, nb, LANES), F32)],
        compiler_params=_params("arbitrary", "arbitrary", "arbitrary"),
        name="moba_attention",
    )(mq.reshape(b, s, width), mk.reshape(b, s, width), mv.reshape(b, s, width), oh)


def _nsa_kernel(q_ref, gate_ref, kc_ref, vc_ref, slw_ref, oh_ref, c2s_ref, o_ref):
    tq = q_ref.shape[1]
    n_cmp = kc_ref.shape[1]
    t0 = pl.program_id(1) * tq
    scale = jnp.asarray(HALF ** -0.5, BF16)

    lane = _iota((tq, LANES), 1)
    lo = lane < HALF
    qb = [q_ref[0, :, LANES * c:LANES * (c + 1)] * scale for c in range(4)]
    qb_sw = [_swap_halves(x) for x in qb]
    zero = jnp.zeros((tq, LANES), BF16)

    def group_queries(g):
        in_g = lo if g == 0 else jnp.logical_not(lo)
        parts = []
        for p in range(4):
            h = 4 * g + p
            x = qb[h // 2] if (h % 2) == g else qb_sw[h // 2]
            parts.append(jnp.where(in_g, x, zero))
        return jnp.concatenate(parts, axis=0), in_g

    qg = [group_queries(g) for g in range(2)]
    q_all = jnp.concatenate([qg[0][0], qg[1][0]], axis=0)

    def chain_rows(x):
        return jnp.concatenate([x] * (FLASH_ROWS // tq), axis=0)

    def rel(n, mult=1):
        return mult * _iota((tq, n), 1) - _iota((tq, n), 0)

    def ones(n):
        return jnp.ones((n, LANES), BF16)

    cbias = chain_rows(jnp.where(rel(n_cmp, NSA_CMP_STRIDE) <= t0 - (NSA_CMP_LEN - 1), 0.0, NEG))
    c2s1 = jnp.concatenate([c2s_ref[...], ones(n_cmp)], axis=1)
    vc_aug = jnp.concatenate([vc_ref[0], c2s1], axis=1)
    o_cmp, i8 = [], []
    for i in range(8 * tq // FLASH_ROWS):
        s = _nt(q_all[i * FLASH_ROWS:(i + 1) * FLASH_ROWS], kc_ref[0]) + cbias
        m = jnp.max(s, axis=1, keepdims=True)
        m = jnp.where(m <= 0.5 * NEG, 0.0, m)
        pf = jnp.exp(s - m)
        p = pf.astype(BF16)
        r = _mm(p, vc_aug)
        r_lo = _mm((pf - p.astype(F32)).astype(BF16), c2s1)
        den = r[:, 2 * LANES:]
        o_cmp.append(r[:, :LANES] * (1.0 / jnp.where(den > 0, den, 1.0)))
        den = den + r_lo[:, LANES:]
        i8.append((r[:, LANES:2 * LANES] + r_lo[:, :LANES]) * (1.0 / jnp.where(den > 0, den, 1.0)))
    o_cmp = jnp.concatenate(o_cmp, axis=0)
    i8 = jnp.concatenate(i8, axis=0)
    imp = jnp.where(lo, (i8[0:tq] + i8[tq:2 * tq]) + (i8[2 * tq:3 * tq] + i8[3 * tq:4 * tq]),
                    (i8[4 * tq:5 * tq] + i8[5 * tq:6 * tq]) + (i8[6 * tq:7 * tq] + i8[7 * tq:8 * tq]))

    imp_t = imp.T
    nblk = NSA_SLC_BLOCK
    jrow = _iota((LANES, tq), 0) & (nblk - 1)
    own = (t0 + _iota((LANES, tq), 1)) >> 6
    forced = (jrow == 0) | (jrow == own) | (jrow == own - 1)
    valid = jrow <= own
    val = jnp.where(valid, jnp.where(forced, jnp.inf, imp_t), -jnp.inf)
    sub = 8
    jloc = _iota((sub, tq), 0)
    cnts = []
    for g in range(2):
        vg = val[nblk * g:nblk * (g + 1)]
        tiles = [vg[sub * r:sub * (r + 1)] for r in range(nblk // sub)]
        cnt_r = [jnp.zeros((sub, tq), F32) for _ in tiles]
        for i in range(nblk):
            vi = jnp.broadcast_to(vg[i:i + 1, :], (sub, tq))
            for r, vr in enumerate(tiles):
                if i < sub * r:
                    beats = vi >= vr
                elif i >= sub * (r + 1):
                    beats = vi > vr
                else:
                    beats = (vi > vr) | ((vi == vr) & (jloc > i - sub * r))
                cnt_r[r] = cnt_r[r] + jnp.where(beats, 1.0, 0.0)
        cnts.extend(cnt_r)
    cnt = jnp.concatenate(cnts, axis=0)
    keep = valid & (cnt < float(NSA_SLC_TOPK))
    bias_q = jnp.where(keep, 0.0, NEG_BIAS).T

    span = NSA_SPAN
    last = pl.program_id(1) // (span // tq)
    bias_g = [jnp.where(qg[g][1], bias_q, 0.0).astype(BF16) for g in range(2)]
    bias_all = jnp.concatenate([bias_g[0]] * 4 + [bias_g[1]] * 4, axis=0)
    q_aug = jnp.concatenate([q_all, bias_all], axis=1)

    def span_kv(j):
        st = pl.multiple_of(j * span, span)
        k_aug = jnp.concatenate([slw_ref[0, pl.ds(st, span), 0:LANES], oh_ref[pl.ds(st, span), :]], axis=1)
        v_aug = jnp.concatenate([slw_ref[0, pl.ds(st, span), LANES:2 * LANES], ones(span)], axis=1)
        return k_aug, v_aug

    sbias = jnp.where(rel(span) <= t0 - last * span, 0.0, NEG)
    carry = _flash_step(q_aug, *span_kv(last), bias=chain_rows(sbias))
    carry = lax.fori_loop(0, last, lambda j, c: _flash_step(q_aug, *span_kv(j), carry=c), carry)
    o_slc = _flash_finish(carry)

    w0 = pl.multiple_of(jnp.maximum(t0 - NSA_WINDOW, 0), tq)
    wspan = NSA_WINDOW + tq
    d = rel(wspan)
    wbias = jnp.where((d <= t0 - w0) & (d > t0 - w0 - NSA_WINDOW), 0.0, NEG)
    o_win = _flash_finish(_flash_step(
        q_all, slw_ref[0, pl.ds(w0, wspan), 2 * LANES:3 * LANES],
        jnp.concatenate([slw_ref[0, pl.ds(w0, wspan), 3 * LANES:4 * LANES], ones(wspan)], axis=1),
        bias=chain_rows(wbias)))

    gate = 1.0 / (1.0 + jnp.exp(-gate_ref[0]))
    heads = []
    for h in range(8):
        r = slice(h * tq, (h + 1) * tq)
        heads.append(gate[:, 3 * h:3 * h + 1] * o_cmp[r]
                     + gate[:, 3 * h + 1:3 * h + 2] * o_slc[r]
                     + gate[:, 3 * h + 2:3 * h + 3] * o_win[r])

    for c in range(4):
        g = c // 2
        a_lo, a_hi = heads[2 * c], heads[2 * c + 1]
        if g == 0:
            a_hi = pltpu.roll(a_hi, HALF, 1)
        else:
            a_lo = pltpu.roll(a_lo, HALF, 1)
        o_ref[0, :, LANES * c:LANES * (c + 1)] = jnp.where(lo, a_lo, a_hi).astype(o_ref.dtype)


def _cmp_to_slc(n_cmp_pad, n_cmp, n_slc):
    rs = NSA_SLC_BLOCK // NSA_CMP_STRIDE
    rc = NSA_CMP_LEN // NSA_CMP_STRIDE
    j = np.arange(n_slc)[:, None, None]
    i = np.broadcast_to(rs * j + np.arange(rs)[None, :, None] - np.arange(rc)[None, None, :],
                        (n_slc, rs, rc))
    jj = np.broadcast_to(j, i.shape)
    ok = (i >= 0) & (i < n_cmp)
    mat = np.zeros((n_cmp, n_slc), np.float32)
    np.add.at(mat, (i[ok], jj[ok]), 1.0)
    out = np.zeros((n_cmp_pad, LANES), np.float32)
    for g in range(2):
        out[:n_cmp, HALF * g:HALF * g + n_slc] = mat
    return jnp.asarray(out, BF16)


def _nsa(nq, ngate, kc, vc, nslw, b, s):
    tq = NSA_TQ
    n_slc = s // NSA_SLC_BLOCK
    n_cmp = (s - NSA_CMP_LEN) // NSA_CMP_STRIDE + 1
    n_cmp_pad = kc.shape[1]
    assert n_slc <= HALF and s >= NSA_WINDOW + tq and s % NSA_SPAN == 0
    width = nq.shape[-1]
    oh = ((jnp.arange(s)[:, None] // NSA_SLC_BLOCK)
          == (jnp.arange(LANES)[None, :] % HALF)).astype(BF16)
    c2s = _cmp_to_slc(n_cmp_pad, n_cmp, n_slc)
    tile = lambda w: pl.BlockSpec((1, tq, w), lambda bi, qt: (bi, qt, 0))
    perb = lambda r, w: pl.BlockSpec((1, r, w), lambda bi, qt: (bi, 0, 0))
    return pl.pallas_call(
        _nsa_kernel,
        grid=(b, s // tq),
        in_specs=[tile(width), tile(LANES), perb(n_cmp_pad, LANES), perb(n_cmp_pad, LANES),
                  perb(s, 4 * LANES), pl.BlockSpec((s, LANES), lambda bi, qt: (0, 0)),
                  pl.BlockSpec(c2s.shape, lambda bi, qt: (0, 0))],
        out_specs=tile(width),
        out_shape=jax.ShapeDtypeStruct((b, s, width), BF16),
        compiler_params=_params("arbitrary", "arbitrary"),
        name="nsa_attention",
    )(nq.reshape(b, s, width), ngate.reshape(b, s, LANES), kc, vc,
      nslw.reshape(b, s, 4 * LANES), oh, c2s)


def _ret_kernel(qk_ref, v_ref, cos_ref, sin_ref, dec_ref, rowdec_ref, cdec_ref, gn_ref, o_ref, r_ref):
    ct = qk_ref.shape[1]
    kw = RET_HEADS * RET_KEY_DIM

    @pl.when(pl.program_id(1) == 0)
    def _():
        r_ref[...] = jnp.zeros_like(r_ref)

    cos = cos_ref[...]
    sin = sin_ref[...]
    lane = _iota((ct, LANES), 1)
    lo = lane < HALF
    first = (lane & (RET_KEY_DIM - 1)) < RET_KEY_DIM // 2

    def rotate(t):
        partner = jnp.where(first, pltpu.roll(t, LANES - RET_KEY_DIM // 2, 1),
                            pltpu.roll(t, RET_KEY_DIM // 2, 1))
        return t * cos + partner * sin

    rowdec = rowdec_ref[...]
    for c in range(RET_HEADS // 2):
        qc = rotate(qk_ref[0, :, LANES * c:LANES * (c + 1)])
        kc = rotate(qk_ref[0, :, kw + LANES * c:kw + LANES * (c + 1)]) * (RET_KEY_DIM ** -0.5)
        kcb = kc.astype(BF16)
        r_old = r_ref[c]
        r_new = r_old * cdec_ref[c]
        r_oldb = r_old.astype(BF16)
        for half in range(2):
            h = 2 * c + half
            in_h = lo if half == 0 else jnp.logical_not(lo)
            qh = jnp.where(in_h, qc, 0.0).astype(BF16)
            vh = v_ref[0, :, RET_VAL_DIM * h:RET_VAL_DIM * (h + 1)]
            sc = (_nt(qh, kcb) * dec_ref[h]).astype(BF16)
            o = _mm(sc, vh) + _mm(qh, r_oldb) * rowdec[:, h:h + 1]
            kd = jnp.where(in_h, kc, 0.0) * rowdec[:, 4 + h:5 + h]
            r_new = r_new + _mm(kd.T.astype(BF16), vh)
            mu = jnp.mean(o, axis=-1, keepdims=True)
            d = o - mu
            var = jnp.mean(d * d, axis=-1, keepdims=True)
            o = d * lax.rsqrt(var + EPS) * gn_ref[:, RET_VAL_DIM * h:RET_VAL_DIM * (h + 1)]
            o_ref[0, :, RET_VAL_DIM * h:RET_VAL_DIM * (h + 1)] = o.astype(o_ref.dtype)
        r_ref[c] = r_new


def _ret_tables(s, ct):
    h, dk = RET_HEADS, RET_KEY_DIM
    gamma = 1.0 - 2.0 ** (-5.0 - np.arange(h))
    log_g = jnp.asarray(np.log(gamma).astype(np.float32))
    inv_freq = jnp.asarray((1.0 / (10000.0 ** np.linspace(0.0, 1.0, dk // 2))).astype(np.float32))
    ang = jnp.arange(s, dtype=F32)[:, None] * inv_freq[None, :]
    cos, sin = jnp.cos(ang), jnp.sin(ang)
    cos_t = jnp.tile(jnp.concatenate([cos, cos], axis=-1), (1, LANES // dk))
    sin_t = jnp.tile(jnp.concatenate([-sin, sin], axis=-1), (1, LANES // dk))
    idx = jnp.arange(ct, dtype=F32)
    diff = idx[:, None] - idx[None, :]
    intra = jnp.where(diff >= 0, jnp.exp(log_g[:, None, None] * jnp.maximum(diff, 0.0)), 0.0)
    cross = jnp.exp(log_g[:, None] * (idx[None, :] + 1.0))
    kdec = jnp.exp(log_g[:, None] * (ct - 1.0 - idx[None, :]))
    rowdec = jnp.zeros((ct, LANES), F32).at[:, 0:h].set(cross.T).at[:, h:2 * h].set(kdec.T)
    cd = jnp.exp(log_g * ct)
    cdec = jnp.broadcast_to(jnp.repeat(cd, HALF).reshape(h // 2, LANES, 1), (h // 2, LANES, LANES))
    return cos_t, sin_t, intra, rowdec, cdec


def _retention(rqk, rv, gn_g, b, s):
    ct = min(RET_TILE, s)
    cos_t, sin_t, intra, rowdec, cdec = _ret_tables(s, ct)
    wqk = rqk.shape[-1]
    wv = rv.shape[-1]
    kw = wqk // 2
    const = lambda a: pl.BlockSpec(a.shape, lambda bi, ci: (0,) * a.ndim)
    return pl.pallas_call(
        _ret_kernel,
        grid=(b, s // ct),
        in_specs=[pl.BlockSpec((1, ct, wqk), lambda bi, ci: (bi, ci, 0)),
                  pl.BlockSpec((1, ct, wv), lambda bi, ci: (bi, ci, 0)),
                  pl.BlockSpec((ct, LANES), lambda bi, ci: (ci, 0)),
                  pl.BlockSpec((ct, LANES), lambda bi, ci: (ci, 0)),
                  const(intra), const(rowdec), const(cdec),
                  pl.BlockSpec((1, wv), lambda bi, ci: (0, 0))],
        out_specs=pl.BlockSpec((1, ct, wv), lambda bi, ci: (bi, ci, 0)),
        out_shape=jax.ShapeDtypeStruct((b, s, wv), BF16),
        scratch_shapes=[pltpu.VMEM((RET_HEADS // 2, LANES, LANES), F32)],
        compiler_params=_params("arbitrary", "arbitrary"),
        name="retention",
    )(rqk.reshape(b, s, wqk), rv.reshape(b, s, wv), cos_t, sin_t, intra, rowdec, cdec,
      gn_g.reshape(1, wv))


def _mem_kernel(q_ref, k_ref, v_ref, o_ref):
    nh = q_ref.shape[2] // LANES
    scale = jnp.asarray(LANES ** -0.5, F32)
    for h in range(nh):
        cs = slice(LANES * h, LANES * (h + 1))
        s = _nt(q_ref[0, :, cs], k_ref[0, :, cs]) * scale
        m = jnp.max(s, axis=1, keepdims=True)
        p = jnp.exp(s - m)
        p = (p / jnp.sum(p, axis=1, keepdims=True)).astype(BF16)
        o_ref[0, :, cs] = _mm(p, v_ref[0, :, cs]).astype(o_ref.dtype)


def _mem_attention(cq, mem_k, mem_v, b, s):
    width = cq.shape[-1]
    m = mem_k.shape[0] // b
    tq = min(MEM_TQ, s)
    return pl.pallas_call(
        _mem_kernel,
        grid=(b, s // tq),
        in_specs=[pl.BlockSpec((1, tq, width), lambda bi, qt: (bi, qt, 0)),
                  pl.BlockSpec((1, m, width), lambda bi, qt: (bi, 0, 0)),
                  pl.BlockSpec((1, m, width), lambda bi, qt: (bi, 0, 0))],
        out_specs=pl.BlockSpec((1, tq, width), lambda bi, qt: (bi, qt, 0)),
        out_shape=jax.ShapeDtypeStruct((b, s, width), BF16),
        compiler_params=_params("arbitrary", "arbitrary"),
        name="memory_attention",
    )(cq.reshape(b, s, width), mem_k.reshape(b, m, width), mem_v.reshape(b, m, width))


def _out_kernel(oa_ref, ob_ref, oc_ref, od_ref, z_ref, x_ref, w_ref, g_ref, y_ref):
    gw = oa_ref.shape[1]
    acc = None
    for i, o_ref in enumerate((oa_ref, ob_ref, oc_ref, od_ref)):
        z = z_ref[:, gw * i:gw * (i + 1)].astype(F32)
        gated = (o_ref[...].astype(F32) * (z * (1.0 / (1.0 + jnp.exp(-z))))).astype(BF16)
        part = _mm(gated, w_ref[gw * i:gw * (i + 1), :])
        acc = part if acc is None else acc + part
    ms = jnp.mean(acc * acc, axis=-1, keepdims=True)
    y_ref[...] = x_ref[...] + acc * lax.rsqrt(ms + EPS) * g_ref[...]


def _out_proj(oa, ob, oc, od, z, x2, w_out, post_g):
    n, d = x2.shape
    gw = oa.shape[-1]
    tm = min(ROW_TILE, n)
    rows = lambda w: pl.BlockSpec((tm, w), lambda i: (i, 0))
    return pl.pallas_call(
        _out_kernel,
        grid=(n // tm,),
        in_specs=[rows(gw), rows(gw), rows(gw), rows(gw), rows(4 * gw), rows(d),
                  pl.BlockSpec(w_out.shape, lambda i: (0, 0)),
                  pl.BlockSpec((1, d), lambda i: (0, 0))],
        out_specs=rows(d),
        out_shape=jax.ShapeDtypeStruct((n, d), F32),
        compiler_params=_params("arbitrary"),
        name="gate_out_proj",
    )(oa.reshape(n, gw), ob.reshape(n, gw), oc.reshape(n, gw), od.reshape(n, gw), z, x2,
      w_out, post_g.reshape(1, d))


def _pad_gate_cols(w_in, gate_off, gate_w):
    d = w_in.shape[0]
    return jnp.concatenate([w_in[:, :gate_off + gate_w],
                            jnp.zeros((d, LANES - gate_w), w_in.dtype),
                            w_in[:, gate_off + gate_w:]], axis=1)


def _layer(x2, mem2, b, s, pre_g, post_g, mem_g, w_in, w_mem_kv,
           pe_k, w1_k, w2_k, pe_v, w1_v, w2_v, ret_gn_g, w_out):
    gw = w_out.shape[0] // 4
    kvw = LANES
    gate_w = 3 * 8
    gate_off = 4 * gw + 6 * kvw
    w = _pad_gate_cols(w_in, gate_off, gate_w).astype(BF16)
    widths = (gw, gw, gw, gw, kvw, kvw, 4 * kvw, LANES, gw, gw, gw, 4 * gw)
    dtypes = (BF16, BF16, BF16, BF16, BF16, BF16, BF16, F32, F32, BF16, BF16, BF16)
    (mq, mk, mv, nq, nkc, nvc, nslw, ngate, rqk, rv, cq, z) = _norm_matmul(
        x2, pre_g, w, widths, dtypes, "norm_in_proj")
    mem_k, mem_v = _norm_matmul(mem2, mem_g, w_mem_kv.astype(BF16), (gw, gw), (BF16, BF16),
                                "norm_mem_kv")

    o_moba = _moba(mq, mk, mv, b, s)
    kc, vc = _compress(nkc, nvc, pe_k, w1_k, w2_k, pe_v, w1_v, w2_v, b, s)
    o_nsa = _nsa(nq, ngate, kc, vc, nslw, b, s)
    o_ret = _retention(rqk, rv, ret_gn_g, b, s)
    o_mem = _mem_attention(cq, mem_k, mem_v, b, s)
    return _out_proj(o_moba, o_nsa, o_ret, o_mem, z, x2, w_out.astype(BF16), post_g)


def kernel(x, mem, pre_norm_g, post_norm_g, mem_norm_g, w_in, w_mem_kv, nsa_pe_k, nsa_w1_k, nsa_w2_k,
           nsa_pe_v, nsa_w1_v, nsa_w2_v, ret_gn_g, w_out):
    b, s, d = x.shape
    x2 = x.reshape(b * s, d)
    mem2 = mem.reshape(b * mem.shape[1], d)
    for l in range(w_in.shape[0]):
        x2 = _layer(x2, mem2, b, s, pre_norm_g[l], post_norm_g[l], mem_norm_g[l], w_in[l], w_mem_kv[l],
                    nsa_pe_k[l], nsa_w1_k[l], nsa_w2_k[l], nsa_pe_v[l], nsa_w1_v[l], nsa_w2_v[l],
                    ret_gn_g[l], w_out[l])
    return x2.reshape(b, s, d)
```

```python
import functools

import numpy as np
import jax
import jax.numpy as jnp
from jax import lax
from jax.experimental import pallas as pl
from jax.experimental.pallas import tpu as pltpu

F32 = jnp.float32
BF16 = jnp.bfloat16

EPS = 1e-6
LANES = 128
HALF = 64
NEG = -1e30
NEG_BIAS = -float(2 ** 30)
VMEM_LIMIT = 56 * 1024 * 1024

FLASH_ROWS = 256
MOBA_BLOCK = 256
MOBA_TOPK = 3
MOBA_GROUP = 4
MOBA_PAIRS = 4
NSA_CMP_LEN = 32
NSA_CMP_STRIDE = 16
NSA_SLC_BLOCK = 64
NSA_SLC_TOPK = 16
NSA_WINDOW = 512
NSA_TQ = 256
NSA_SPAN = 1024
RET_HEADS = 4
RET_KEY_DIM = 64
RET_VAL_DIM = 128
RET_CHUNK = 256
RET_TILE = 1024
MEM_TQ = 512
ROW_TILE = 512


def _nt(a, b):
    return lax.dot_general(a, b, (((1,), (1,)), ((), ())), preferred_element_type=F32)


def _mm(a, b):
    return jnp.dot(a, b, preferred_element_type=F32)


def _iota(shape, dim):
    return lax.broadcasted_iota(jnp.int32, shape, dim)


def _swap_halves(x):
    return pltpu.roll(x.astype(F32), HALF, 1).astype(x.dtype)


def _params(*sem):
    return pltpu.CompilerParams(dimension_semantics=sem, vmem_limit_bytes=VMEM_LIMIT)


def _norm_matmul_kernel(x_ref, g_ref, w_ref, *out_refs, widths):
    x = x_ref[...]
    ms = jnp.mean(x * x, axis=-1, keepdims=True)
    h = (x * lax.rsqrt(ms + EPS) * g_ref[...]).astype(BF16)
    off = 0
    for o_ref, wd in zip(out_refs, widths):
        o_ref[...] = _mm(h, w_ref[:, off:off + wd]).astype(o_ref.dtype)
        off += wd


def _norm_matmul(x2, g, w, widths, dtypes, name):
    n, d = x2.shape
    tm = min(ROW_TILE, n)
    nc = w.shape[1]
    assert sum(widths) == nc and n % tm == 0
    return pl.pallas_call(
        functools.partial(_norm_matmul_kernel, widths=tuple(widths)),
        grid=(n // tm,),
        in_specs=[pl.BlockSpec((tm, d), lambda i: (i, 0)),
                  pl.BlockSpec((1, d), lambda i: (0, 0)),
                  pl.BlockSpec((d, nc), lambda i: (0, 0))],
        out_specs=[pl.BlockSpec((tm, wd), lambda i: (i, 0)) for wd in widths],
        out_shape=[jax.ShapeDtypeStruct((n, wd), dt) for wd, dt in zip(widths, dtypes)],
        compiler_params=_params("arbitrary"),
        name=name,
    )(x2, g.reshape(1, d), w)


def _compress_kernel(xk_ref, xv_ref, pek_ref, pev_ref, wak_ref, wbk_ref, w2k_ref,
                     wav_ref, wbv_ref, w2v_ref, kc_ref, vc_ref):
    def one(x_ref, pe_ref, wa_ref, wb_ref, w2_ref, o_ref):
        x = x_ref[0].astype(F32)
        pe = pe_ref[...]
        u = _mm((x + pe[0:1, :]).astype(BF16), wa_ref[...])
        v = _mm((x + pe[1:2, :]).astype(BF16), wb_ref[...])
        n = u.shape[0]
        hid = u + pltpu.roll(v, n - 1, 0)
        hid = hid * (1.0 / (1.0 + jnp.exp(-hid)))
        o_ref[0] = _mm(hid.astype(BF16), w2_ref[...]).astype(o_ref.dtype)

    one(xk_ref, pek_ref, wak_ref, wbk_ref, w2k_ref, kc_ref)
    one(xv_ref, pev_ref, wav_ref, wbv_ref, w2v_ref, vc_ref)


def _compress_weights(pe, w1, w2):
    l, dh = pe.shape
    hidden = w1.shape[1]
    half = l // 2
    eye = jnp.eye(2, dtype=F32)
    w1r = w1.reshape(2, half, dh, hidden)
    wab = jnp.einsum('sodj,gh->sogdhj', w1r, eye).reshape(2, half * 2 * dh, 2 * hidden)
    w2b = jnp.einsum('jd,gh->gjhd', w2, eye).reshape(2 * hidden, 2 * dh)
    pe2 = jnp.broadcast_to(pe.reshape(2, half, 1, dh), (2, half, 2, dh)).reshape(2, half * 2 * dh)
    return pe2, wab[0].astype(BF16), wab[1].astype(BF16), w2b.astype(BF16)


def _compress(nkc, nvc, pe_k, w1_k, w2_k, pe_v, w1_v, w2_v, b, s):
    rows = s // NSA_CMP_STRIDE
    width = NSA_CMP_STRIDE * LANES
    xk = nkc.reshape(b, rows, width)
    xv = nvc.reshape(b, rows, width)
    pk, wak, wbk, w2k = _compress_weights(pe_k, w1_k, w2_k)
    pv, wav, wbv, w2v = _compress_weights(pe_v, w1_v, w2_v)
    hid2 = wak.shape[1]
    xspec = pl.BlockSpec((1, rows, width), lambda i: (i, 0, 0))
    full = lambda a: pl.BlockSpec(a.shape, lambda i: (0,) * a.ndim)
    ospec = pl.BlockSpec((1, rows, LANES), lambda i: (i, 0, 0))
    return pl.pallas_call(
        _compress_kernel,
        grid=(b,),
        in_specs=[xspec, xspec, full(pk), full(pv), full(wak), full(wbk), full(w2k),
                  full(wav), full(wbv), full(w2v)],
        out_specs=[ospec, ospec],
        out_shape=[jax.ShapeDtypeStruct((b, rows, LANES), BF16)] * 2,
        compiler_params=_params("arbitrary"),
        name="nsa_compress",
    )(xk, xv, pk, pv, wak, wbk, w2k, wav, wbv, w2v)


def _flash_step(q_aug, k_aug, v_aug, carry=None, bias=None):
    out = []
    for i in range(q_aug.shape[0] // FLASH_ROWS):
        s = _nt(q_aug[i * FLASH_ROWS:(i + 1) * FLASH_ROWS], k_aug)
        if bias is not None:
            nb = bias.shape[1]
            s = s + bias if nb == s.shape[1] else jnp.concatenate([s[:, :nb] + bias, s[:, nb:]], axis=1)
        m = jnp.max(s, axis=1, keepdims=True)
        if carry is None:
            acc = _mm(jnp.exp(s - m).astype(BF16), v_aug)
        else:
            m_old = carry[2 * i]
            m = jnp.maximum(m_old, m)
            acc = jnp.exp(m_old - m) * carry[2 * i + 1] + _mm(jnp.exp(s - m).astype(BF16), v_aug)
        out.extend((m, acc))
    return tuple(out)


def _flash_finish(carry):
    return jnp.concatenate([acc[:, :LANES] / acc[:, LANES:] for acc in carry[1::2]], axis=0)


def _moba_kernel(q_ref, k_ref, v_ref, oh_ref, o_ref, kmean_ref):
    blk = q_ref.shape[1]
    npair = q_ref.shape[2] // LANES
    nb = k_ref.shape[1] // blk
    grp = MOBA_GROUP
    qt = pl.program_id(2)

    @pl.when(qt == 0)
    def _():
        for c in range(npair):
            for j in range(nb):
                kj = k_ref[0, j * blk:(j + 1) * blk, LANES * c:LANES * (c + 1)].astype(F32)
                kmean_ref[c, j:j + 1, :] = jnp.sum(kj, axis=0, keepdims=True) * (1.0 / blk)

    lane = _iota((blk, LANES), 1)
    lo = lane < HALF
    causal = jnp.where(_iota((blk, blk), 1) <= _iota((blk, blk), 0), 0.0, NEG)
    ones = jnp.ones((grp * blk, LANES), BF16)
    zero = jnp.zeros((blk, LANES), BF16)

    q_aug = []
    for c in range(npair):
        q = q_ref[0, :, LANES * c:LANES * (c + 1)]
        q2 = jnp.concatenate([jnp.where(lo, q, zero), jnp.where(lo, zero, q)], axis=0)
        km = kmean_ref[c]
        km_hi = km.astype(BF16)
        gt = _nt(km_hi, q2) + _nt((km - km_hi.astype(F32)).astype(BF16), q2)
        jidx = _iota(gt.shape, 0)
        cnt = jnp.zeros(gt.shape, F32)
        for i in range(nb):
            gi = gt[i:i + 1, :]
            beats = (gi > gt) | ((gi == gt) & (jidx > i))
            past_i = jnp.full(gt.shape, i, jnp.int32) < qt
            cnt = cnt + jnp.where(beats & past_i, 1.0, 0.0)
        keep = ((jidx < qt) & (cnt < float(MOBA_TOPK))) | (jidx == qt)
        bias_t = jnp.concatenate([jnp.where(keep, 0.0, NEG_BIAS),
                                  jnp.zeros((LANES - nb - 1, 2 * blk), F32),
                                  jnp.full((1, 2 * blk), NEG_BIAS, F32)], axis=0)
        q_aug.append(jnp.concatenate([q2 * jnp.asarray(HALF ** -0.5, BF16),
                                      bias_t.T.astype(BF16)], axis=1))

    def group_kv(c, gi):
        ks, vs = [], []
        for t in range(grp):
            j = qt - (gi * grp + t)
            sk = pl.multiple_of(jnp.maximum(j, 0) * blk, blk)
            so = pl.multiple_of(jnp.where(j >= 0, j, nb) * blk, blk)
            ks.append(jnp.concatenate([k_ref[0, pl.ds(sk, blk), LANES * c:LANES * (c + 1)],
                                       oh_ref[pl.ds(so, blk), :]], axis=1))
            vs.append(v_ref[0, pl.ds(sk, blk), LANES * c:LANES * (c + 1)])
        return jnp.concatenate(ks, axis=0), jnp.concatenate([jnp.concatenate(vs, axis=0), ones], axis=1)

    def run(n_groups):
        def branch():
            for c in range(npair):
                carry = _flash_step(q_aug[c], *group_kv(c, 0), bias=causal)
                for gi in range(1, n_groups):
                    carry = _flash_step(q_aug[c], *group_kv(c, gi), carry=carry)
                out = _flash_finish(carry)
                o_ref[0, :, LANES * c:LANES * (c + 1)] = (
                    jnp.where(lo, out[:blk], out[blk:]).astype(o_ref.dtype))
        return branch

    lax.switch(qt // grp, [run(n) for n in range(1, nb // grp + 1)])


def _moba(mq, mk, mv, b, s):
    width = mq.shape[-1]
    blk = MOBA_BLOCK
    nb = s // blk
    pw = LANES * MOBA_PAIRS
    assert nb < LANES - 1 and nb % MOBA_GROUP == 0 and width % pw == 0
    blk_id = jnp.where(jnp.arange(s + blk) < s, jnp.arange(s + blk) // blk, LANES - 1)
    oh = (blk_id[:, None] == jnp.arange(LANES)[None, :]).astype(BF16)
    qspec = pl.BlockSpec((1, blk, pw), lambda bi, hp, qt: (bi, qt, hp))
    kspec = pl.BlockSpec((1, s, pw), lambda bi, hp, qt: (bi, 0, hp))
    return pl.pallas_call(
        _moba_kernel,
        grid=(b, width // pw, nb),
        in_specs=[qspec, kspec, kspec, pl.BlockSpec((s + blk, LANES), lambda bi, hp, qt: (0, 0))],
        out_specs=qspec,
        out_shape=jax.ShapeDtypeStruct((b, s, width), BF16),
        scratch_shapes=[pltpu.VMEM((MOBA_PAIRS, nb, LANES), F32)],
        compiler_params=_params("arbitrary", "arbitrary", "arbitrary"),
        name="moba_attention",
    )(mq.reshape(b, s, width), mk.reshape(b, s, width), mv.reshape(b, s, width), oh)


def _nsa_kernel(q_ref, gate_ref, kc_ref, vc_ref, slw_ref, oh_ref, c2s_ref, o_ref):
    tq = q_ref.shape[1]
    n_cmp = kc_ref.shape[1]
    t0 = pl.program_id(1) * tq
    scale = jnp.asarray(HALF ** -0.5, BF16)

    lane = _iota((tq, LANES), 1)
    lo = lane < HALF
    qb = [q_ref[0, :, LANES * c:LANES * (c + 1)] * scale for c in range(4)]
    qb_sw = [_swap_halves(x) for x in qb]
    zero = jnp.zeros((tq, LANES), BF16)

    def group_queries(g):
        in_g = lo if g == 0 else jnp.logical_not(lo)
        parts = []
        for p in range(4):
            h = 4 * g + p
            x = qb[h // 2] if (h % 2) == g else qb_sw[h // 2]
            parts.append(jnp.where(in_g, x, zero))
        return jnp.concatenate(parts, axis=0), in_g

    qg = [group_queries(g) for g in range(2)]
    q_all = jnp.concatenate([qg[0][0], qg[1][0]], axis=0)

    def chain_rows(x):
        return jnp.concatenate([x] * (FLASH_ROWS // tq), axis=0)

    def rel(n, mult=1):
        return mult * _iota((tq, n), 1) - _iota((tq, n), 0)

    def ones(n):
        return jnp.ones((n, LANES), BF16)

    cbias = chain_rows(jnp.where(rel(n_cmp, NSA_CMP_STRIDE) <= t0 - (NSA_CMP_LEN - 1), 0.0, NEG))
    c2s1 = jnp.concatenate([c2s_ref[...], ones(n_cmp)], axis=1)
    vc_aug = jnp.concatenate([vc_ref[0], c2s1], axis=1)
    o_cmp, i8 = [], []
    for i in range(8 * tq // FLASH_ROWS):
        s = _nt(q_all[i * FLASH_ROWS:(i + 1) * FLASH_ROWS], kc_ref[0]) + cbias
        m = jnp.max(s, axis=1, keepdims=True)
        m = jnp.where(m <= 0.5 * NEG, 0.0, m)
        pf = jnp.exp(s - m)
        p = pf.astype(BF16)
        r = _mm(p, vc_aug)
        r_lo = _mm((pf - p.astype(F32)).astype(BF16), c2s1)
        den = r[:, 2 * LANES:]
        o_cmp.append(r[:, :LANES] * (1.0 / jnp.where(den > 0, den, 1.0)))
        den = den + r_lo[:, LANES:]
        i8.append((r[:, LANES:2 * LANES] + r_lo[:, :LANES]) * (1.0 / jnp.where(den > 0, den, 1.0)))
    o_cmp = jnp.concatenate(o_cmp, axis=0)
    i8 = jnp.concatenate(i8, axis=0)
    imp = jnp.where(lo, (i8[0:tq] + i8[tq:2 * tq]) + (i8[2 * tq:3 * tq] + i8[3 * tq:4 * tq]),
                    (i8[4 * tq:5 * tq] + i8[5 * tq:6 * tq]) + (i8[6 * tq:7 * tq] + i8[7 * tq:8 * tq]))

    imp_t = imp.T
    nblk = NSA_SLC_BLOCK
    jrow = _iota((LANES, tq), 0) & (nblk - 1)
    own = (t0 + _iota((LANES, tq), 1)) >> 6
    forced = (jrow == 0) | (jrow == own) | (jrow == own - 1)
    valid = jrow <= own
    val = jnp.where(valid, jnp.where(forced, jnp.inf, imp_t), -jnp.inf)
    sub = 8
    jloc = _iota((sub, tq), 0)
    cnts = []
    for g in range(2):
        vg = val[nblk * g:nblk * (g + 1)]
        tiles = [vg[sub * r:sub * (r + 1)] for r in range(nblk // sub)]
        cnt_r = [jnp.zeros((sub, tq), F32) for _ in tiles]
        for i in range(nblk):
            vi = jnp.broadcast_to(vg[i:i + 1, :], (sub, tq))
            for r, vr in enumerate(tiles):
                if i < sub * r:
                    beats = vi >= vr
                elif i >= sub * (r + 1):
                    beats = vi > vr
                else:
                    beats = (vi > vr) | ((vi == vr) & (jloc > i - sub * r))
                cnt_r[r] = cnt_r[r] + jnp.where(beats, 1.0, 0.0)
        cnts.extend(cnt_r)
    cnt = jnp.concatenate(cnts, axis=0)
    keep = valid & (cnt < float(NSA_SLC_TOPK))
    bias_q = jnp.where(keep, 0.0, NEG_BIAS).T

    span = NSA_SPAN
    bias_g = [jnp.where(qg[g][1], bias_q, 0.0).astype(BF16) for g in range(2)]
    bias_all = jnp.concatenate([bias_g[0]] * 4 + [bias_g[1]] * 4, axis=0)
    q_aug = jnp.concatenate([q_all, bias_all], axis=1)
    gate = 1.0 / (1.0 + jnp.exp(-gate_ref[0]))
    w0 = pl.multiple_of(jnp.maximum(t0 - NSA_WINDOW, 0), tq)
    wspan = NSA_WINDOW + tq

    def span_kv(j):
        rows = slice(j * span, (j + 1) * span)
        k_aug = jnp.concatenate([slw_ref[0, rows, 0:LANES], oh_ref[rows, :]], axis=1)
        v_aug = jnp.concatenate([slw_ref[0, rows, LANES:2 * LANES], ones(span)], axis=1)
        return k_aug, v_aug

    def run(n_spans):
        def branch():
            last = n_spans - 1
            sbias = jnp.where(rel(span) <= t0 - last * span, 0.0, NEG)
            carry = _flash_step(q_aug, *span_kv(last), bias=chain_rows(sbias))
            for j in range(last):
                carry = _flash_step(q_aug, *span_kv(j), carry=carry)
            o_slc = _flash_finish(carry)

            d = rel(wspan)
            wbias = jnp.where((d <= t0 - w0) & (d > t0 - w0 - NSA_WINDOW), 0.0, NEG)
            o_win = _flash_finish(_flash_step(
                q_all, slw_ref[0, pl.ds(w0, wspan), 2 * LANES:3 * LANES],
                jnp.concatenate([slw_ref[0, pl.ds(w0, wspan), 3 * LANES:4 * LANES], ones(wspan)], axis=1),
                bias=chain_rows(wbias)))

            heads = []
            for h in range(8):
                r = slice(h * tq, (h + 1) * tq)
                heads.append(gate[:, 3 * h:3 * h + 1] * o_cmp[r]
                             + gate[:, 3 * h + 1:3 * h + 2] * o_slc[r]
                             + gate[:, 3 * h + 2:3 * h + 3] * o_win[r])
            for c in range(4):
                a_lo, a_hi = heads[2 * c], heads[2 * c + 1]
                if c // 2 == 0:
                    a_hi = pltpu.roll(a_hi, HALF, 1)
                else:
                    a_lo = pltpu.roll(a_lo, HALF, 1)
                o_ref[0, :, LANES * c:LANES * (c + 1)] = jnp.where(lo, a_lo, a_hi).astype(o_ref.dtype)
        return branch

    lax.switch(pl.program_id(1) // (span // tq), [run(n) for n in range(1, slw_ref.shape[1] // span + 1)])


def _cmp_to_slc(n_cmp_pad, n_cmp, n_slc):
    rs = NSA_SLC_BLOCK // NSA_CMP_STRIDE
    rc = NSA_CMP_LEN // NSA_CMP_STRIDE
    j = np.arange(n_slc)[:, None, None]
    i = np.broadcast_to(rs * j + np.arange(rs)[None, :, None] - np.arange(rc)[None, None, :],
                        (n_slc, rs, rc))
    jj = np.broadcast_to(j, i.shape)
    ok = (i >= 0) & (i < n_cmp)
    mat = np.zeros((n_cmp, n_slc), np.float32)
    np.add.at(mat, (i[ok], jj[ok]), 1.0)
    out = np.zeros((n_cmp_pad, LANES), np.float32)
    for g in range(2):
        out[:n_cmp, HALF * g:HALF * g + n_slc] = mat
    return jnp.asarray(out, BF16)


def _nsa(nq, ngate, kc, vc, nslw, b, s):
    tq = NSA_TQ
    n_slc = s // NSA_SLC_BLOCK
    n_cmp = (s - NSA_CMP_LEN) // NSA_CMP_STRIDE + 1
    n_cmp_pad = kc.shape[1]
    assert n_slc <= HALF and s >= NSA_WINDOW + tq and s % NSA_SPAN == 0
    width = nq.shape[-1]
    oh = ((jnp.arange(s)[:, None] // NSA_SLC_BLOCK)
          == (jnp.arange(LANES)[None, :] % HALF)).astype(BF16)
    c2s = _cmp_to_slc(n_cmp_pad, n_cmp, n_slc)
    tile = lambda w: pl.BlockSpec((1, tq, w), lambda bi, qt: (bi, qt, 0))
    perb = lambda r, w: pl.BlockSpec((1, r, w), lambda bi, qt: (bi, 0, 0))
    return pl.pallas_call(
        _nsa_kernel,
        grid=(b, s // tq),
        in_specs=[tile(width), tile(LANES), perb(n_cmp_pad, LANES), perb(n_cmp_pad, LANES),
                  perb(s, 4 * LANES), pl.BlockSpec((s, LANES), lambda bi, qt: (0, 0)),
                  pl.BlockSpec(c2s.shape, lambda bi, qt: (0, 0))],
        out_specs=tile(width),
        out_shape=jax.ShapeDtypeStruct((b, s, width), BF16),
        compiler_params=_params("arbitrary", "arbitrary"),
        name="nsa_attention",
    )(nq.reshape(b, s, width), ngate.reshape(b, s, LANES), kc, vc,
      nslw.reshape(b, s, 4 * LANES), oh, c2s)


def _ret_kernel(qk_ref, v_ref, cos_ref, sin_ref, dec_ref, rowdec_ref, cdec_ref, gn_ref, o_ref, r_ref):
    ct = qk_ref.shape[1]
    kw = RET_HEADS * RET_KEY_DIM

    @pl.when(pl.program_id(1) == 0)
    def _():
        r_ref[...] = jnp.zeros_like(r_ref)

    cc = dec_ref.shape[1]
    lane = _iota((cc, LANES), 1)
    lo = lane < HALF
    first = (lane & (RET_KEY_DIM - 1)) < RET_KEY_DIM // 2
    rowdec = rowdec_ref[...]
    r_state = [r_ref[c] for c in range(RET_HEADS // 2)]

    for sub in range(ct // cc):
        rows = slice(sub * cc, (sub + 1) * cc)
        cos = cos_ref[rows, :]
        sin = sin_ref[rows, :]

        def rotate(t):
            partner = jnp.where(first, pltpu.roll(t, LANES - RET_KEY_DIM // 2, 1),
                                pltpu.roll(t, RET_KEY_DIM // 2, 1))
            return t * cos + partner * sin

        for c in range(RET_HEADS // 2):
            qc = rotate(qk_ref[0, rows, LANES * c:LANES * (c + 1)])
            kc = rotate(qk_ref[0, rows, kw + LANES * c:kw + LANES * (c + 1)]) * (RET_KEY_DIM ** -0.5)
            kcb = kc.astype(BF16)
            r_old = r_state[c]
            r_new = r_old * cdec_ref[c]
            r_oldb = r_old.astype(BF16)
            for half in range(2):
                h = 2 * c + half
                in_h = lo if half == 0 else jnp.logical_not(lo)
                qh = jnp.where(in_h, qc, 0.0).astype(BF16)
                vh = v_ref[0, rows, RET_VAL_DIM * h:RET_VAL_DIM * (h + 1)]
                sc = (_nt(qh, kcb) * dec_ref[h]).astype(BF16)
                o = _mm(sc, vh) + _mm(qh, r_oldb) * rowdec[:, h:h + 1]
                kd = jnp.where(in_h, kc, 0.0) * rowdec[:, 4 + h:5 + h]
                r_new = r_new + _mm(kd.T.astype(BF16), vh)
                mu = jnp.mean(o, axis=-1, keepdims=True)
                d = o - mu
                var = jnp.mean(d * d, axis=-1, keepdims=True)
                o = d * lax.rsqrt(var + EPS) * gn_ref[:, RET_VAL_DIM * h:RET_VAL_DIM * (h + 1)]
                o_ref[0, rows, RET_VAL_DIM * h:RET_VAL_DIM * (h + 1)] = o.astype(o_ref.dtype)
            r_state[c] = r_new

    for c in range(RET_HEADS // 2):
        r_ref[c] = r_state[c]


def _ret_tables(s, ct):
    h, dk = RET_HEADS, RET_KEY_DIM
    gamma = 1.0 - 2.0 ** (-5.0 - np.arange(h))
    log_g = jnp.asarray(np.log(gamma).astype(np.float32))
    inv_freq = jnp.asarray((1.0 / (10000.0 ** np.linspace(0.0, 1.0, dk // 2))).astype(np.float32))
    ang = jnp.arange(s, dtype=F32)[:, None] * inv_freq[None, :]
    cos, sin = jnp.cos(ang), jnp.sin(ang)
    cos_t = jnp.tile(jnp.concatenate([cos, cos], axis=-1), (1, LANES // dk))
    sin_t = jnp.tile(jnp.concatenate([-sin, sin], axis=-1), (1, LANES // dk))
    idx = jnp.arange(ct, dtype=F32)
    diff = idx[:, None] - idx[None, :]
    intra = jnp.where(diff >= 0, jnp.exp(log_g[:, None, None] * jnp.maximum(diff, 0.0)), 0.0)
    cross = jnp.exp(log_g[:, None] * (idx[None, :] + 1.0))
    kdec = jnp.exp(log_g[:, None] * (ct - 1.0 - idx[None, :]))
    rowdec = jnp.zeros((ct, LANES), F32).at[:, 0:h].set(cross.T).at[:, h:2 * h].set(kdec.T)
    cd = jnp.exp(log_g * ct)
    cdec = jnp.broadcast_to(jnp.repeat(cd, HALF).reshape(h // 2, LANES, 1), (h // 2, LANES, LANES))
    return cos_t, sin_t, intra, rowdec, cdec


def _retention(rqk, rv, gn_g, b, s):
    ct = min(RET_TILE, s)
    cos_t, sin_t, intra, rowdec, cdec = _ret_tables(s, min(RET_CHUNK, ct))
    wqk = rqk.shape[-1]
    wv = rv.shape[-1]
    kw = wqk // 2
    const = lambda a: pl.BlockSpec(a.shape, lambda bi, ci: (0,) * a.ndim)
    return pl.pallas_call(
        _ret_kernel,
        grid=(b, s // ct),
        in_specs=[pl.BlockSpec((1, ct, wqk), lambda bi, ci: (bi, ci, 0)),
                  pl.BlockSpec((1, ct, wv), lambda bi, ci: (bi, ci, 0)),
                  pl.BlockSpec((ct, LANES), lambda bi, ci: (ci, 0)),
                  pl.BlockSpec((ct, LANES), lambda bi, ci: (ci, 0)),
                  const(intra), const(rowdec), const(cdec),
                  pl.BlockSpec((1, wv), lambda bi, ci: (0, 0))],
        out_specs=pl.BlockSpec((1, ct, wv), lambda bi, ci: (bi, ci, 0)),
        out_shape=jax.ShapeDtypeStruct((b, s, wv), BF16),
        scratch_shapes=[pltpu.VMEM((RET_HEADS // 2, LANES, LANES), F32)],
        compiler_params=_params("arbitrary", "arbitrary"),
        name="retention",
    )(rqk.reshape(b, s, wqk), rv.reshape(b, s, wv), cos_t, sin_t, intra, rowdec, cdec,
      gn_g.reshape(1, wv))


def _mem_kernel(q_ref, k_ref, v_ref, o_ref):
    nh = q_ref.shape[2] // LANES
    scale = jnp.asarray(LANES ** -0.5, F32)
    for h in range(nh):
        cs = slice(LANES * h, LANES * (h + 1))
        s = _nt(q_ref[0, :, cs], k_ref[0, :, cs]) * scale
        m = jnp.max(s, axis=1, keepdims=True)
        p = jnp.exp(s - m)
        p = (p / jnp.sum(p, axis=1, keepdims=True)).astype(BF16)
        o_ref[0, :, cs] = _mm(p, v_ref[0, :, cs]).astype(o_ref.dtype)


def _mem_attention(cq, mem_k, mem_v, b, s):
    width = cq.shape[-1]
    m = mem_k.shape[0] // b
    tq = min(MEM_TQ, s)
    return pl.pallas_call(
        _mem_kernel,
        grid=(b, s // tq),
        in_specs=[pl.BlockSpec((1, tq, width), lambda bi, qt: (bi, qt, 0)),
                  pl.BlockSpec((1, m, width), lambda bi, qt: (bi, 0, 0)),
                  pl.BlockSpec((1, m, width), lambda bi, qt: (bi, 0, 0))],
        out_specs=pl.BlockSpec((1, tq, width), lambda bi, qt: (bi, qt, 0)),
        out_shape=jax.ShapeDtypeStruct((b, s, width), BF16),
        compiler_params=_params("arbitrary", "arbitrary"),
        name="memory_attention",
    )(cq.reshape(b, s, width), mem_k.reshape(b, m, width), mem_v.reshape(b, m, width))


def _out_kernel(oa_ref, ob_ref, oc_ref, od_ref, z_ref, x_ref, w_ref, g_ref, y_ref):
    gw = oa_ref.shape[1]
    acc = None
    for i, o_ref in enumerate((oa_ref, ob_ref, oc_ref, od_ref)):
        z = z_ref[:, gw * i:gw * (i + 1)].astype(F32)
        gated = (o_ref[...].astype(F32) * (z * (1.0 / (1.0 + jnp.exp(-z))))).astype(BF16)
        part = _mm(gated, w_ref[gw * i:gw * (i + 1), :])
        acc = part if acc is None else acc + part
    ms = jnp.mean(acc * acc, axis=-1, keepdims=True)
    y_ref[...] = x_ref[...] + acc * lax.rsqrt(ms + EPS) * g_ref[...]


def _out_proj(oa, ob, oc, od, z, x2, w_out, post_g):
    n, d = x2.shape
    gw = oa.shape[-1]
    tm = min(ROW_TILE, n)
    rows = lambda w: pl.BlockSpec((tm, w), lambda i: (i, 0))
    return pl.pallas_call(
        _out_kernel,
        grid=(n // tm,),
        in_specs=[rows(gw), rows(gw), rows(gw), rows(gw), rows(4 * gw), rows(d),
                  pl.BlockSpec(w_out.shape, lambda i: (0, 0)),
                  pl.BlockSpec((1, d), lambda i: (0, 0))],
        out_specs=rows(d),
        out_shape=jax.ShapeDtypeStruct((n, d), F32),
        compiler_params=_params("arbitrary"),
        name="gate_out_proj",
    )(oa.reshape(n, gw), ob.reshape(n, gw), oc.reshape(n, gw), od.reshape(n, gw), z, x2,
      w_out, post_g.reshape(1, d))


def _pad_gate_cols(w_in, gate_off, gate_w):
    d = w_in.shape[0]
    return jnp.concatenate([w_in[:, :gate_off + gate_w],
                            jnp.zeros((d, LANES - gate_w), w_in.dtype),
                            w_in[:, gate_off + gate_w:]], axis=1)


def _layer(x2, mem2, b, s, pre_g, post_g, mem_g, w_in, w_mem_kv,
           pe_k, w1_k, w2_k, pe_v, w1_v, w2_v, ret_gn_g, w_out):
    gw = w_out.shape[0] // 4
    kvw = LANES
    gate_w = 3 * 8
    gate_off = 4 * gw + 6 * kvw
    w = _pad_gate_cols(w_in, gate_off, gate_w).astype(BF16)
    widths = (gw, gw, gw, gw, kvw, kvw, 4 * kvw, LANES, gw, gw, gw, 4 * gw)
    dtypes = (BF16, BF16, BF16, BF16, BF16, BF16, BF16, F32, F32, BF16, BF16, BF16)
    (mq, mk, mv, nq, nkc, nvc, nslw, ngate, rqk, rv, cq, z) = _norm_matmul(
        x2, pre_g, w, widths, dtypes, "norm_in_proj")
    mem_k, mem_v = _norm_matmul(mem2, mem_g, w_mem_kv.astype(BF16), (gw, gw), (BF16, BF16),
                                "norm_mem_kv")

    o_moba = _moba(mq, mk, mv, b, s)
    kc, vc = _compress(nkc, nvc, pe_k, w1_k, w2_k, pe_v, w1_v, w2_v, b, s)
    o_nsa = _nsa(nq, ngate, kc, vc, nslw, b, s)
    o_ret = _retention(rqk, rv, ret_gn_g, b, s)
    o_mem = _mem_attention(cq, mem_k, mem_v, b, s)
    return _out_proj(o_moba, o_nsa, o_ret, o_mem, z, x2, w_out.astype(BF16), post_g)


def kernel(x, mem, pre_norm_g, post_norm_g, mem_norm_g, w_in, w_mem_kv, nsa_pe_k, nsa_w1_k, nsa_w2_k,
           nsa_pe_v, nsa_w1_v, nsa_w2_v, ret_gn_g, w_out):
    b, s, d = x.shape
    x2 = x.reshape(b * s, d)
    mem2 = mem.reshape(b * mem.shape[1], d)
    for l in range(w_in.shape[0]):
        x2 = _layer(x2, mem2, b, s, pre_norm_g[l], post_norm_g[l], mem_norm_g[l], w_in[l], w_mem_kv[l],
                    nsa_pe_k[l], nsa_w1_k[l], nsa_w2_k[l], nsa_pe_v[l], nsa_w1_v[l], nsa_w2_v[l],
                    ret_gn_g[l], w_out[l])
    return x2.reshape(b, s, d)
```

```python
import functools

import numpy as np
import jax
import jax.numpy as jnp
from jax import lax
from jax.experimental import pallas as pl
from jax.experimental.pallas import tpu as pltpu

F32 = jnp.float32
BF16 = jnp.bfloat16

EPS = 1e-6
LANES = 128
HALF = 64
NEG = -float(2 ** 100)
NEG_BIAS = NEG
VMEM_LIMIT = 56 * 1024 * 1024

FLASH_ROWS = 256
MOBA_BLOCK = 256
MOBA_TOPK = 3
MOBA_GROUP = 4
MOBA_PAIRS = 4
NSA_CMP_LEN = 32
NSA_CMP_STRIDE = 16
NSA_SLC_BLOCK = 64
NSA_SLC_TOPK = 16
NSA_WINDOW = 512
NSA_TQ = 256
NSA_SPAN = 1024
RET_HEADS = 4
RET_KEY_DIM = 64
RET_VAL_DIM = 128
RET_CHUNK = 256
RET_TILE = 1024
MEM_TQ = 512
ROW_TILE = 512


def _nt(a, b):
    return lax.dot_general(a, b, (((1,), (1,)), ((), ())), preferred_element_type=F32)


def _mm(a, b):
    return jnp.dot(a, b, preferred_element_type=F32)


def _iota(shape, dim):
    return lax.broadcasted_iota(jnp.int32, shape, dim)


def _swap_halves(x):
    return pltpu.roll(x.astype(F32), HALF, 1).astype(x.dtype)


def _params(*sem):
    return pltpu.CompilerParams(dimension_semantics=sem, vmem_limit_bytes=VMEM_LIMIT)


def _norm_matmul_kernel(x_ref, g_ref, w_ref, *out_refs, widths):
    x = x_ref[...]
    ms = jnp.mean(x * x, axis=-1, keepdims=True)
    h = (x * lax.rsqrt(ms + EPS) * g_ref[...]).astype(BF16)
    off = 0
    for o_ref, wd in zip(out_refs, widths):
        o_ref[...] = _mm(h, w_ref[:, off:off + wd]).astype(o_ref.dtype)
        off += wd


def _norm_matmul(x2, g, w, widths, dtypes, name):
    n, d = x2.shape
    tm = min(ROW_TILE, n)
    nc = w.shape[1]
    assert sum(widths) == nc and n % tm == 0
    return pl.pallas_call(
        functools.partial(_norm_matmul_kernel, widths=tuple(widths)),
        grid=(n // tm,),
        in_specs=[pl.BlockSpec((tm, d), lambda i: (i, 0)),
                  pl.BlockSpec((1, d), lambda i: (0, 0)),
                  pl.BlockSpec((d, nc), lambda i: (0, 0))],
        out_specs=[pl.BlockSpec((tm, wd), lambda i: (i, 0)) for wd in widths],
        out_shape=[jax.ShapeDtypeStruct((n, wd), dt) for wd, dt in zip(widths, dtypes)],
        compiler_params=_params("arbitrary"),
        name=name,
    )(x2, g.reshape(1, d), w)


def _compress_kernel(xk_ref, xv_ref, pek_ref, pev_ref, wak_ref, wbk_ref, w2k_ref,
                     wav_ref, wbv_ref, w2v_ref, kc_ref, vc_ref):
    def one(x_ref, pe_ref, wa_ref, wb_ref, w2_ref, o_ref):
        x = x_ref[0].astype(F32)
        pe = pe_ref[...]
        u = _mm((x + pe[0:1, :]).astype(BF16), wa_ref[...])
        v = _mm((x + pe[1:2, :]).astype(BF16), wb_ref[...])
        n = u.shape[0]
        hid = u + pltpu.roll(v, n - 1, 0)
        hid = hid * (1.0 / (1.0 + jnp.exp(-hid)))
        o_ref[0] = _mm(hid.astype(BF16), w2_ref[...]).astype(o_ref.dtype)

    one(xk_ref, pek_ref, wak_ref, wbk_ref, w2k_ref, kc_ref)
    one(xv_ref, pev_ref, wav_ref, wbv_ref, w2v_ref, vc_ref)


def _compress_weights(pe, w1, w2):
    l, dh = pe.shape
    hidden = w1.shape[1]
    half = l // 2
    eye = jnp.eye(2, dtype=F32)
    w1r = w1.reshape(2, half, dh, hidden)
    wab = jnp.einsum('sodj,gh->sogdhj', w1r, eye).reshape(2, half * 2 * dh, 2 * hidden)
    w2b = jnp.einsum('jd,gh->gjhd', w2, eye).reshape(2 * hidden, 2 * dh)
    pe2 = jnp.broadcast_to(pe.reshape(2, half, 1, dh), (2, half, 2, dh)).reshape(2, half * 2 * dh)
    return pe2, wab[0].astype(BF16), wab[1].astype(BF16), w2b.astype(BF16)


def _compress(nkc, nvc, pe_k, w1_k, w2_k, pe_v, w1_v, w2_v, b, s):
    rows = s // NSA_CMP_STRIDE
    width = NSA_CMP_STRIDE * LANES
    xk = nkc.reshape(b, rows, width)
    xv = nvc.reshape(b, rows, width)
    pk, wak, wbk, w2k = _compress_weights(pe_k, w1_k, w2_k)
    pv, wav, wbv, w2v = _compress_weights(pe_v, w1_v, w2_v)
    hid2 = wak.shape[1]
    xspec = pl.BlockSpec((1, rows, width), lambda i: (i, 0, 0))
    full = lambda a: pl.BlockSpec(a.shape, lambda i: (0,) * a.ndim)
    ospec = pl.BlockSpec((1, rows, LANES), lambda i: (i, 0, 0))
    return pl.pallas_call(
        _compress_kernel,
        grid=(b,),
        in_specs=[xspec, xspec, full(pk), full(pv), full(wak), full(wbk), full(w2k),
                  full(wav), full(wbv), full(w2v)],
        out_specs=[ospec, ospec],
        out_shape=[jax.ShapeDtypeStruct((b, rows, LANES), BF16)] * 2,
        compiler_params=_params("arbitrary"),
        name="nsa_compress",
    )(xk, xv, pk, pv, wak, wbk, w2k, wav, wbv, w2v)


def _flash_step(q_aug, k_aug, v_aug, carry=None, bias=None):
    out = []
    for i in range(q_aug.shape[0] // FLASH_ROWS):
        s = _nt(q_aug[i * FLASH_ROWS:(i + 1) * FLASH_ROWS], k_aug)
        if bias is not None:
            r0 = (i * FLASH_ROWS) % bias.shape[0]
            bi = bias[r0:r0 + FLASH_ROWS]
            nb = bi.shape[1]
            s = s + bi if nb == s.shape[1] else jnp.concatenate([s[:, :nb] + bi, s[:, nb:]], axis=1)
        m = jnp.max(s, axis=1, keepdims=True)
        if carry is None:
            acc = _mm(jnp.exp(s - m).astype(BF16), v_aug)
        else:
            m_old = carry[2 * i]
            m = jnp.maximum(m_old, m)
            acc = jnp.exp(m_old - m) * carry[2 * i + 1] + _mm(jnp.exp(s - m).astype(BF16), v_aug)
        out.extend((m, acc))
    return tuple(out)


def _flash_finish(carry):
    return jnp.concatenate([acc[:, :LANES] / acc[:, LANES:] for acc in carry[1::2]], axis=0)


def _moba_kernel(q_ref, k_ref, v_ref, oh_ref, o_ref, kmean_ref):
    blk = q_ref.shape[1]
    npair = q_ref.shape[2] // LANES
    nb = k_ref.shape[1] // blk
    grp = MOBA_GROUP
    qt = pl.program_id(2)

    @pl.when(qt == 0)
    def _():
        for c in range(npair):
            for j in range(nb):
                kj = k_ref[0, j * blk:(j + 1) * blk, LANES * c:LANES * (c + 1)].astype(F32)
                kmean_ref[c, j:j + 1, :] = jnp.sum(kj, axis=0, keepdims=True) * (1.0 / blk)

    lane = _iota((blk, LANES), 1)
    lo = lane < HALF
    causal = jnp.where(_iota((blk, blk), 1) <= _iota((blk, blk), 0), 0.0, NEG)
    ones = jnp.ones((grp * blk, LANES), BF16)
    zero = jnp.zeros((blk, LANES), BF16)

    q_aug = []
    for c in range(npair):
        q = q_ref[0, :, LANES * c:LANES * (c + 1)]
        q2 = jnp.concatenate([jnp.where(lo, q, zero), jnp.where(lo, zero, q)], axis=0)
        km = kmean_ref[c]
        km_hi = km.astype(BF16)
        gt = _nt(km_hi, q2) + _nt((km - km_hi.astype(F32)).astype(BF16), q2)
        jidx = _iota(gt.shape, 0)
        cnt = jnp.zeros(gt.shape, F32)
        for i in range(nb):
            gi = gt[i:i + 1, :]
            beats = (gi > gt) | ((gi == gt) & (jidx > i))
            past_i = jnp.full(gt.shape, i, jnp.int32) < qt
            cnt = cnt + jnp.where(beats & past_i, 1.0, 0.0)
        keep = ((jidx < qt) & (cnt < float(MOBA_TOPK))) | (jidx == qt)
        bias_t = jnp.concatenate([jnp.where(keep, 0.0, NEG_BIAS),
                                  jnp.zeros((LANES - nb - 1, 2 * blk), F32),
                                  jnp.full((1, 2 * blk), NEG_BIAS, F32)], axis=0)
        q_aug.append(jnp.concatenate([q2 * jnp.asarray(HALF ** -0.5, BF16),
                                      bias_t.T.astype(BF16)], axis=1))

    def group_kv(c, gi):
        ks, vs = [], []
        for t in range(grp):
            j = qt - (gi * grp + t)
            sk = pl.multiple_of(jnp.maximum(j, 0) * blk, blk)
            so = pl.multiple_of(jnp.where(j >= 0, j, nb) * blk, blk)
            ks.append(jnp.concatenate([k_ref[0, pl.ds(sk, blk), LANES * c:LANES * (c + 1)],
                                       oh_ref[pl.ds(so, blk), :]], axis=1))
            vs.append(v_ref[0, pl.ds(sk, blk), LANES * c:LANES * (c + 1)])
        return jnp.concatenate(ks, axis=0), jnp.concatenate([jnp.concatenate(vs, axis=0), ones], axis=1)

    def run(n_groups):
        def branch():
            for c in range(npair):
                carry = _flash_step(q_aug[c], *group_kv(c, 0), bias=causal)
                for gi in range(1, n_groups):
                    carry = _flash_step(q_aug[c], *group_kv(c, gi), carry=carry)
                out = _flash_finish(carry)
                o_ref[0, :, LANES * c:LANES * (c + 1)] = (
                    jnp.where(lo, out[:blk], out[blk:]).astype(o_ref.dtype))
        return branch

    lax.switch(qt // grp, [run(n) for n in range(1, nb // grp + 1)])


def _moba(mq, mk, mv, b, s):
    width = mq.shape[-1]
    blk = MOBA_BLOCK
    nb = s // blk
    pw = LANES * MOBA_PAIRS
    assert nb < LANES - 1 and nb % MOBA_GROUP == 0 and width % pw == 0
    blk_id = jnp.where(jnp.arange(s + blk) < s, jnp.arange(s + blk) // blk, LANES - 1)
    oh = (blk_id[:, None] == jnp.arange(LANES)[None, :]).astype(BF16)
    qspec = pl.BlockSpec((1, blk, pw), lambda bi, hp, qt: (bi, qt, hp))
    kspec = pl.BlockSpec((1, s, pw), lambda bi, hp, qt: (bi, 0, hp))
    return pl.pallas_call(
        _moba_kernel,
        grid=(b, width // pw, nb),
        in_specs=[qspec, kspec, kspec, pl.BlockSpec((s + blk, LANES), lambda bi, hp, qt: (0, 0))],
        out_specs=qspec,
        out_shape=jax.ShapeDtypeStruct((b, s, width), BF16),
        scratch_shapes=[pltpu.VMEM((MOBA_PAIRS, nb, LANES), F32)],
        compiler_params=_params("arbitrary", "arbitrary", "arbitrary"),
        name="moba_attention",
    )(mq.reshape(b, s, width), mk.reshape(b, s, width), mv.reshape(b, s, width), oh)


def _nsa_kernel(q_ref, gate_ref, kc_ref, vc_ref, slw_ref, oh_ref, c2s_ref, o_ref):
    tq = q_ref.shape[1]
    n_cmp = kc_ref.shape[1]
    t0 = pl.program_id(1) * tq
    scale = jnp.asarray(HALF ** -0.5, BF16)

    lane = _iota((tq, LANES), 1)
    lo = lane < HALF
    qb = [q_ref[0, :, LANES * c:LANES * (c + 1)] * scale for c in range(4)]
    qb_sw = [_swap_halves(x) for x in qb]
    zero = jnp.zeros((tq, LANES), BF16)

    def group_queries(g):
        in_g = lo if g == 0 else jnp.logical_not(lo)
        parts = []
        for p in range(4):
            h = 4 * g + p
            x = qb[h // 2] if (h % 2) == g else qb_sw[h // 2]
            parts.append(jnp.where(in_g, x, zero))
        return jnp.concatenate(parts, axis=0), in_g

    qg = [group_queries(g) for g in range(2)]
    q_all = jnp.concatenate([qg[0][0], qg[1][0]], axis=0)

    def chain_rows(x):
        return jnp.concatenate([x] * max(1, FLASH_ROWS // tq), axis=0)

    def rel(n, mult=1):
        return mult * _iota((tq, n), 1) - _iota((tq, n), 0)

    def ones(n):
        return jnp.ones((n, LANES), BF16)

    cbias = chain_rows(jnp.where(rel(n_cmp, NSA_CMP_STRIDE) <= t0 - (NSA_CMP_LEN - 1), 0.0, NEG))
    vc_aug = jnp.concatenate([vc_ref[0], c2s_ref[...], ones(n_cmp)], axis=1)
    o_cmp, i8 = [], []
    for i in range(8 * tq // FLASH_ROWS):
        r0 = (i * FLASH_ROWS) % cbias.shape[0]
        s = _nt(q_all[i * FLASH_ROWS:(i + 1) * FLASH_ROWS], kc_ref[0]) + cbias[r0:r0 + FLASH_ROWS]
        m = jnp.max(s, axis=1, keepdims=True)
        m = jnp.where(m <= 0.5 * NEG, 0.0, m)
        r = _mm(jnp.exp(s - m).astype(BF16), vc_aug)
        den = r[:, 2 * LANES:]
        inv = 1.0 / jnp.where(den > 0, den, 1.0)
        o_cmp.append(r[:, :LANES] * inv)
        i8.append(r[:, LANES:2 * LANES] * inv)
    o_cmp = jnp.concatenate(o_cmp, axis=0)
    i8 = jnp.concatenate(i8, axis=0)
    imp = jnp.where(lo, (i8[0:tq] + i8[tq:2 * tq]) + (i8[2 * tq:3 * tq] + i8[3 * tq:4 * tq]),
                    (i8[4 * tq:5 * tq] + i8[5 * tq:6 * tq]) + (i8[6 * tq:7 * tq] + i8[7 * tq:8 * tq]))

    imp_t = imp.T
    nblk = NSA_SLC_BLOCK
    jrow = _iota((LANES, tq), 0) & (nblk - 1)
    own = (t0 + _iota((LANES, tq), 1)) >> 6
    forced = (jrow == 0) | (jrow == own) | (jrow == own - 1)
    valid = jrow <= own
    val = jnp.where(valid, jnp.where(forced, jnp.inf, imp_t), -jnp.inf)
    sub = 8
    jloc = _iota((sub, tq), 0)
    cnts = []
    for g in range(2):
        vg = val[nblk * g:nblk * (g + 1)]
        tiles = [vg[sub * r:sub * (r + 1)] for r in range(nblk // sub)]
        cnt_r = [jnp.zeros((sub, tq), F32) for _ in tiles]
        for i in range(nblk):
            vi = jnp.broadcast_to(vg[i:i + 1, :], (sub, tq))
            for r, vr in enumerate(tiles):
                if i < sub * r:
                    beats = vi >= vr
                elif i >= sub * (r + 1):
                    beats = vi > vr
                else:
                    beats = (vi > vr) | ((vi == vr) & (jloc > i - sub * r))
                cnt_r[r] = cnt_r[r] + jnp.where(beats, 1.0, 0.0)
        cnts.extend(cnt_r)
    cnt = jnp.concatenate(cnts, axis=0)
    keep = valid & (cnt < float(NSA_SLC_TOPK))
    bias_q = jnp.where(keep, 0.0, NEG_BIAS).T

    span = NSA_SPAN
    bias_g = [jnp.where(qg[g][1], bias_q, 0.0).astype(BF16) for g in range(2)]
    bias_all = jnp.concatenate([bias_g[0]] * 4 + [bias_g[1]] * 4, axis=0)
    q_aug = jnp.concatenate([q_all, bias_all], axis=1)
    gate = 1.0 / (1.0 + jnp.exp(-gate_ref[0]))
    w0 = pl.multiple_of(jnp.maximum(t0 - NSA_WINDOW, 0), tq)
    wspan = NSA_WINDOW + tq

    last = pl.program_id(1) // (span // tq)

    def span_kv(j):
        rows = pl.ds(pl.multiple_of(j * span, span), span)
        k_aug = jnp.concatenate([slw_ref[0, rows, 0:LANES], oh_ref[rows, :]], axis=1)
        v_aug = jnp.concatenate([slw_ref[0, rows, LANES:2 * LANES], ones(span)], axis=1)
        return k_aug, v_aug

    def newest_step():
        sbias = jnp.where(rel(span) <= t0 - last * span, 0.0, NEG)
        return _flash_step(q_aug, *span_kv(last), bias=chain_rows(sbias))

    def window():
        d = rel(wspan)
        wbias = jnp.where((d <= t0 - w0) & (d > t0 - w0 - NSA_WINDOW), 0.0, NEG)
        return _flash_finish(_flash_step(
            q_all, slw_ref[0, pl.ds(w0, wspan), 2 * LANES:3 * LANES],
            jnp.concatenate([slw_ref[0, pl.ds(w0, wspan), 3 * LANES:4 * LANES], ones(wspan)], axis=1),
            bias=chain_rows(wbias)))

    def combine(o_slc, o_win):
        heads = []
        for h in range(8):
            r = slice(h * tq, (h + 1) * tq)
            heads.append(gate[:, 3 * h:3 * h + 1] * o_cmp[r]
                         + gate[:, 3 * h + 1:3 * h + 2] * o_slc[r]
                         + gate[:, 3 * h + 2:3 * h + 3] * o_win[r])
        for c in range(4):
            a_lo, a_hi = heads[2 * c], heads[2 * c + 1]
            if c // 2 == 0:
                a_hi = pltpu.roll(a_hi, HALF, 1)
            else:
                a_lo = pltpu.roll(a_lo, HALF, 1)
            o_ref[0, :, LANES * c:LANES * (c + 1)] = jnp.where(lo, a_lo, a_hi).astype(o_ref.dtype)

    carry = newest_step()
    o_win = window()
    carry = lax.fori_loop(0, last, lambda j, c: _flash_step(q_aug, *span_kv(j), carry=c), carry)
    combine(_flash_finish(carry), o_win)


def _cmp_to_slc(n_cmp_pad, n_cmp, n_slc):
    rs = NSA_SLC_BLOCK // NSA_CMP_STRIDE
    rc = NSA_CMP_LEN // NSA_CMP_STRIDE
    j = np.arange(n_slc)[:, None, None]
    i = np.broadcast_to(rs * j + np.arange(rs)[None, :, None] - np.arange(rc)[None, None, :],
                        (n_slc, rs, rc))
    jj = np.broadcast_to(j, i.shape)
    ok = (i >= 0) & (i < n_cmp)
    mat = np.zeros((n_cmp, n_slc), np.float32)
    np.add.at(mat, (i[ok], jj[ok]), 1.0)
    out = np.zeros((n_cmp_pad, LANES), np.float32)
    for g in range(2):
        out[:n_cmp, HALF * g:HALF * g + n_slc] = mat
    return jnp.asarray(out, BF16)


def _nsa(nq, ngate, kc, vc, nslw, b, s):
    tq = NSA_TQ
    n_slc = s // NSA_SLC_BLOCK
    n_cmp = (s - NSA_CMP_LEN) // NSA_CMP_STRIDE + 1
    n_cmp_pad = kc.shape[1]
    assert n_slc <= HALF and s >= NSA_WINDOW + tq and s % NSA_SPAN == 0
    width = nq.shape[-1]
    oh = ((jnp.arange(s)[:, None] // NSA_SLC_BLOCK)
          == (jnp.arange(LANES)[None, :] % HALF)).astype(BF16)
    c2s = _cmp_to_slc(n_cmp_pad, n_cmp, n_slc)
    tile = lambda w: pl.BlockSpec((1, tq, w), lambda bi, qt: (bi, qt, 0))
    perb = lambda r, w: pl.BlockSpec((1, r, w), lambda bi, qt: (bi, 0, 0))
    return pl.pallas_call(
        _nsa_kernel,
        grid=(b, s // tq),
        in_specs=[tile(width), tile(LANES), perb(n_cmp_pad, LANES), perb(n_cmp_pad, LANES),
                  perb(s, 4 * LANES), pl.BlockSpec((s, LANES), lambda bi, qt: (0, 0)),
                  pl.BlockSpec(c2s.shape, lambda bi, qt: (0, 0))],
        out_specs=tile(width),
        out_shape=jax.ShapeDtypeStruct((b, s, width), BF16),
        compiler_params=_params("arbitrary", "arbitrary"),
        name="nsa_attention",
    )(nq.reshape(b, s, width), ngate.reshape(b, s, LANES), kc, vc,
      nslw.reshape(b, s, 4 * LANES), oh, c2s)


def _ret_kernel(qk_ref, v_ref, cos_ref, sin_ref, dec_ref, rowdec_ref, cdec_ref, gn_ref, o_ref, r_ref):
    ct = qk_ref.shape[1]
    kw = RET_HEADS * RET_KEY_DIM

    @pl.when(pl.program_id(1) == 0)
    def _():
        r_ref[...] = jnp.zeros_like(r_ref)

    cc = dec_ref.shape[1]
    lane = _iota((cc, LANES), 1)
    lo = lane < HALF
    first = (lane & (RET_KEY_DIM - 1)) < RET_KEY_DIM // 2
    rowdec = rowdec_ref[...]
    r_state = [r_ref[c] for c in range(RET_HEADS // 2)]

    for sub in range(ct // cc):
        rows = slice(sub * cc, (sub + 1) * cc)
        cos = cos_ref[rows, :]
        sin = sin_ref[rows, :]

        def rotate(t):
            partner = jnp.where(first, pltpu.roll(t, LANES - RET_KEY_DIM // 2, 1),
                                pltpu.roll(t, RET_KEY_DIM // 2, 1))
            return t * cos + partner * sin

        for c in range(RET_HEADS // 2):
            qc = rotate(qk_ref[0, rows, LANES * c:LANES * (c + 1)])
            kc = rotate(qk_ref[0, rows, kw + LANES * c:kw + LANES * (c + 1)]) * (RET_KEY_DIM ** -0.5)
            kcb = kc.astype(BF16)
            r_old = r_state[c]
            r_new = r_old * cdec_ref[c]
            r_oldb = r_old.astype(BF16)
            for half in range(2):
                h = 2 * c + half
                in_h = lo if half == 0 else jnp.logical_not(lo)
                qh = jnp.where(in_h, qc, 0.0).astype(BF16)
                vh = v_ref[0, rows, RET_VAL_DIM * h:RET_VAL_DIM * (h + 1)]
                sc = (_nt(qh, kcb) * dec_ref[h]).astype(BF16)
                o = _mm(sc, vh) + _mm(qh, r_oldb) * rowdec[:, h:h + 1]
                kd = jnp.where(in_h, kc, 0.0) * rowdec[:, 4 + h:5 + h]
                r_new = r_new + _mm(kd.T.astype(BF16), vh)
                mu = jnp.mean(o, axis=-1, keepdims=True)
                d = o - mu
                var = jnp.mean(d * d, axis=-1, keepdims=True)
                o = d * lax.rsqrt(var + EPS) * gn_ref[:, RET_VAL_DIM * h:RET_VAL_DIM * (h + 1)]
                o_ref[0, rows, RET_VAL_DIM * h:RET_VAL_DIM * (h + 1)] = o.astype(o_ref.dtype)
            r_state[c] = r_new

    for c in range(RET_HEADS // 2):
        r_ref[c] = r_state[c]


def _ret_tables(s, ct):
    h, dk = RET_HEADS, RET_KEY_DIM
    gamma = 1.0 - 2.0 ** (-5.0 - np.arange(h))
    log_g = jnp.asarray(np.log(gamma).astype(np.float32))
    inv_freq = jnp.asarray((1.0 / (10000.0 ** np.linspace(0.0, 1.0, dk // 2))).astype(np.float32))
    ang = jnp.arange(s, dtype=F32)[:, None] * inv_freq[None, :]
    cos, sin = jnp.cos(ang), jnp.sin(ang)
    cos_t = jnp.tile(jnp.concatenate([cos, cos], axis=-1), (1, LANES // dk))
    sin_t = jnp.tile(jnp.concatenate([-sin, sin], axis=-1), (1, LANES // dk))
    idx = jnp.arange(ct, dtype=F32)
    diff = idx[:, None] - idx[None, :]
    intra = jnp.where(diff >= 0, jnp.exp(log_g[:, None, None] * jnp.maximum(diff, 0.0)), 0.0)
    cross = jnp.exp(log_g[:, None] * (idx[None, :] + 1.0))
    kdec = jnp.exp(log_g[:, None] * (ct - 1.0 - idx[None, :]))
    rowdec = jnp.zeros((ct, LANES), F32).at[:, 0:h].set(cross.T).at[:, h:2 * h].set(kdec.T)
    cd = jnp.exp(log_g * ct)
    cdec = jnp.broadcast_to(jnp.repeat(cd, HALF).reshape(h // 2, LANES, 1), (h // 2, LANES, LANES))
    return cos_t, sin_t, intra, rowdec, cdec


def _retention(rqk, rv, gn_g, b, s):
    ct = min(RET_TILE, s)
    cos_t, sin_t, intra, rowdec, cdec = _ret_tables(s, min(RET_CHUNK, ct))
    wqk = rqk.shape[-1]
    wv = rv.shape[-1]
    kw = wqk // 2
    const = lambda a: pl.BlockSpec(a.shape, lambda bi, ci: (0,) * a.ndim)
    return pl.pallas_call(
        _ret_kernel,
        grid=(b, s // ct),
        in_specs=[pl.BlockSpec((1, ct, wqk), lambda bi, ci: (bi, ci, 0)),
                  pl.BlockSpec((1, ct, wv), lambda bi, ci: (bi, ci, 0)),
                  pl.BlockSpec((ct, LANES), lambda bi, ci: (ci, 0)),
                  pl.BlockSpec((ct, LANES), lambda bi, ci: (ci, 0)),
                  const(intra), const(rowdec), const(cdec),
                  pl.BlockSpec((1, wv), lambda bi, ci: (0, 0))],
        out_specs=pl.BlockSpec((1, ct, wv), lambda bi, ci: (bi, ci, 0)),
        out_shape=jax.ShapeDtypeStruct((b, s, wv), BF16),
        scratch_shapes=[pltpu.VMEM((RET_HEADS // 2, LANES, LANES), F32)],
        compiler_params=_params("arbitrary", "arbitrary"),
        name="retention",
    )(rqk.reshape(b, s, wqk), rv.reshape(b, s, wv), cos_t, sin_t, intra, rowdec, cdec,
      gn_g.reshape(1, wv))


def _mem_kernel(q_ref, k_ref, v_ref, o_ref):
    nh = q_ref.shape[2] // LANES
    scale = jnp.asarray(LANES ** -0.5, F32)
    for h in range(nh):
        cs = slice(LANES * h, LANES * (h + 1))
        s = _nt(q_ref[0, :, cs], k_ref[0, :, cs]) * scale
        m = jnp.max(s, axis=1, keepdims=True)
        p = jnp.exp(s - m)
        p = (p / jnp.sum(p, axis=1, keepdims=True)).astype(BF16)
        o_ref[0, :, cs] = _mm(p, v_ref[0, :, cs]).astype(o_ref.dtype)


def _mem_attention(cq, mem_k, mem_v, b, s):
    width = cq.shape[-1]
    m = mem_k.shape[0] // b
    tq = min(MEM_TQ, s)
    return pl.pallas_call(
        _mem_kernel,
        grid=(b, s // tq),
        in_specs=[pl.BlockSpec((1, tq, width), lambda bi, qt: (bi, qt, 0)),
                  pl.BlockSpec((1, m, width), lambda bi, qt: (bi, 0, 0)),
                  pl.BlockSpec((1, m, width), lambda bi, qt: (bi, 0, 0))],
        out_specs=pl.BlockSpec((1, tq, width), lambda bi, qt: (bi, qt, 0)),
        out_shape=jax.ShapeDtypeStruct((b, s, width), BF16),
        compiler_params=_params("arbitrary", "arbitrary"),
        name="memory_attention",
    )(cq.reshape(b, s, width), mem_k.reshape(b, m, width), mem_v.reshape(b, m, width))


def _out_kernel(oa_ref, ob_ref, oc_ref, od_ref, z_ref, x_ref, w_ref, g_ref, y_ref):
    gw = oa_ref.shape[1]
    acc = None
    for i, o_ref in enumerate((oa_ref, ob_ref, oc_ref, od_ref)):
        z = z_ref[:, gw * i:gw * (i + 1)].astype(F32)
        gated = (o_ref[...].astype(F32) * (z * (1.0 / (1.0 + jnp.exp(-z))))).astype(BF16)
        part = _mm(gated, w_ref[gw * i:gw * (i + 1), :])
        acc = part if acc is None else acc + part
    ms = jnp.mean(acc * acc, axis=-1, keepdims=True)
    y_ref[...] = x_ref[...] + acc * lax.rsqrt(ms + EPS) * g_ref[...]


def _out_proj(oa, ob, oc, od, z, x2, w_out, post_g):
    n, d = x2.shape
    gw = oa.shape[-1]
    tm = min(ROW_TILE, n)
    rows = lambda w: pl.BlockSpec((tm, w), lambda i: (i, 0))
    return pl.pallas_call(
        _out_kernel,
        grid=(n // tm,),
        in_specs=[rows(gw), rows(gw), rows(gw), rows(gw), rows(4 * gw), rows(d),
                  pl.BlockSpec(w_out.shape, lambda i: (0, 0)),
                  pl.BlockSpec((1, d), lambda i: (0, 0))],
        out_specs=rows(d),
        out_shape=jax.ShapeDtypeStruct((n, d), F32),
        compiler_params=_params("arbitrary"),
        name="gate_out_proj",
    )(oa.reshape(n, gw), ob.reshape(n, gw), oc.reshape(n, gw), od.reshape(n, gw), z, x2,
      w_out, post_g.reshape(1, d))


def _pad_gate_cols(w_in, gate_off, gate_w):
    d = w_in.shape[0]
    return jnp.concatenate([w_in[:, :gate_off + gate_w],
                            jnp.zeros((d, LANES - gate_w), w_in.dtype),
                            w_in[:, gate_off + gate_w:]], axis=1)


def _layer(x2, mem2, b, s, pre_g, post_g, mem_g, w_in, w_mem_kv,
           pe_k, w1_k, w2_k, pe_v, w1_v, w2_v, ret_gn_g, w_out):
    gw = w_out.shape[0] // 4
    kvw = LANES
    gate_w = 3 * 8
    gate_off = 4 * gw + 6 * kvw
    w = _pad_gate_cols(w_in, gate_off, gate_w).astype(BF16)
    widths = (gw, gw, gw, gw, kvw, kvw, 4 * kvw, LANES, gw, gw, gw, 4 * gw)
    dtypes = (BF16, BF16, BF16, BF16, BF16, BF16, BF16, F32, F32, BF16, BF16, BF16)
    (mq, mk, mv, nq, nkc, nvc, nslw, ngate, rqk, rv, cq, z) = _norm_matmul(
        x2, pre_g, w, widths, dtypes, "norm_in_proj")
    mem_k, mem_v = _norm_matmul(mem2, mem_g, w_mem_kv.astype(BF16), (gw, gw), (BF16, BF16),
                                "norm_mem_kv")

    o_moba = _moba(mq, mk, mv, b, s)
    kc, vc = _compress(nkc, nvc, pe_k, w1_k, w2_k, pe_v, w1_v, w2_v, b, s)
    o_nsa = _nsa(nq, ngate, kc, vc, nslw, b, s)
    o_ret = _retention(rqk, rv, ret_gn_g, b, s)
    o_mem = _mem_attention(cq, mem_k, mem_v, b, s)
    return _out_proj(o_moba, o_nsa, o_ret, o_mem, z, x2, w_out.astype(BF16), post_g)


def kernel(x, mem, pre_norm_g, post_norm_g, mem_norm_g, w_in, w_mem_kv, nsa_pe_k, nsa_w1_k, nsa_w2_k,
           nsa_pe_v, nsa_w1_v, nsa_w2_v, ret_gn_g, w_out):
    b, s, d = x.shape
    x2 = x.reshape(b * s, d)
    mem2 = mem.reshape(b * mem.shape[1], d)
    for l in range(w_in.shape[0]):
        x2 = _layer(x2, mem2, b, s, pre_norm_g[l], post_norm_g[l], mem_norm_g[l], w_in[l], w_mem_kv[l],
                    nsa_pe_k[l], nsa_w1_k[l], nsa_w2_k[l], nsa_pe_v[l], nsa_w1_v[l], nsa_w2_v[l],
                    ret_gn_g[l], w_out[l])
    return x2.reshape(b, s, d)
```

```python
import functools

import numpy as np
import jax
import jax.numpy as jnp
from jax import lax
from jax.experimental import pallas as pl
from jax.experimental.pallas import tpu as pltpu

F32 = jnp.float32
BF16 = jnp.bfloat16

EPS = 1e-6
LANES = 128
HALF = 64
NEG = -float(2 ** 100)
NEG_BIAS = NEG
V7X_VMEM_BYTES = 64 * 1024 * 1024
VMEM_LIMIT = V7X_VMEM_BYTES * 7 // 8

FLASH_ROWS = 256
MOBA_BLOCK = 256
MOBA_TOPK = 3
MOBA_GROUP = 4
MOBA_PAIRS = 4
NSA_CMP_LEN = 32
NSA_CMP_STRIDE = 16
NSA_SLC_BLOCK = 64
NSA_SLC_TOPK = 16
NSA_WINDOW = 512
NSA_TQ = 256
NSA_SPAN = 1024
RET_HEADS = 4
RET_KEY_DIM = 64
RET_VAL_DIM = 128
RET_CHUNK = 256
RET_TILE = 1024
MEM_TQ = 512
ROW_TILE = 512


def _nt(a, b):
    return lax.dot_general(a, b, (((1,), (1,)), ((), ())), preferred_element_type=F32)


def _mm(a, b):
    return jnp.dot(a, b, preferred_element_type=F32)


def _iota(shape, dim):
    return lax.broadcasted_iota(jnp.int32, shape, dim)


def _swap_halves(x):
    return pltpu.roll(x.astype(F32), HALF, 1).astype(x.dtype)


def _params(*sem):
    return pltpu.CompilerParams(dimension_semantics=sem, vmem_limit_bytes=VMEM_LIMIT)


def _norm_matmul_kernel(x_ref, g_ref, w_ref, *out_refs, widths):
    x = x_ref[...]
    ms = jnp.mean(x * x, axis=-1, keepdims=True)
    h = (x * lax.rsqrt(ms + EPS) * g_ref[...]).astype(BF16)
    off = 0
    for o_ref, wd in zip(out_refs, widths):
        o_ref[...] = _mm(h, w_ref[:, off:off + wd]).astype(o_ref.dtype)
        off += wd


def _norm_matmul(x2, g, w, widths, dtypes, name):
    n, d = x2.shape
    tm = min(ROW_TILE, n)
    nc = w.shape[1]
    assert sum(widths) == nc and n % tm == 0
    return pl.pallas_call(
        functools.partial(_norm_matmul_kernel, widths=tuple(widths)),
        grid=(n // tm,),
        in_specs=[pl.BlockSpec((tm, d), lambda i: (i, 0)),
                  pl.BlockSpec((1, d), lambda i: (0, 0)),
                  pl.BlockSpec((d, nc), lambda i: (0, 0))],
        out_specs=[pl.BlockSpec((tm, wd), lambda i: (i, 0)) for wd in widths],
        out_shape=[jax.ShapeDtypeStruct((n, wd), dt) for wd, dt in zip(widths, dtypes)],
        compiler_params=_params("arbitrary"),
        name=name,
    )(x2, g.reshape(1, d), w)


def _compress_kernel(xk_ref, xv_ref, pek_ref, pev_ref, wak_ref, wbk_ref, w2k_ref,
                     wav_ref, wbv_ref, w2v_ref, kc_ref, vc_ref):
    def one(x_ref, pe_ref, wa_ref, wb_ref, w2_ref, o_ref):
        x = x_ref[0].astype(F32)
        pe = pe_ref[...]
        u = _mm((x + pe[0:1, :]).astype(BF16), wa_ref[...])
        v = _mm((x + pe[1:2, :]).astype(BF16), wb_ref[...])
        n = u.shape[0]
        hid = u + pltpu.roll(v, n - 1, 0)
        hid = hid * (1.0 / (1.0 + jnp.exp(-hid)))
        o_ref[0] = _mm(hid.astype(BF16), w2_ref[...]).astype(o_ref.dtype)

    one(xk_ref, pek_ref, wak_ref, wbk_ref, w2k_ref, kc_ref)
    one(xv_ref, pev_ref, wav_ref, wbv_ref, w2v_ref, vc_ref)


def _compress_weights(pe, w1, w2):
    l, dh = pe.shape
    hidden = w1.shape[1]
    half = l // 2
    eye = jnp.eye(2, dtype=F32)
    w1r = w1.reshape(2, half, dh, hidden)
    wab = jnp.einsum('sodj,gh->sogdhj', w1r, eye).reshape(2, half * 2 * dh, 2 * hidden)
    w2b = jnp.einsum('jd,gh->gjhd', w2, eye).reshape(2 * hidden, 2 * dh)
    pe2 = jnp.broadcast_to(pe.reshape(2, half, 1, dh), (2, half, 2, dh)).reshape(2, half * 2 * dh)
    return pe2, wab[0].astype(BF16), wab[1].astype(BF16), w2b.astype(BF16)


def _compress(nkc, nvc, pe_k, w1_k, w2_k, pe_v, w1_v, w2_v, b, s):
    rows = s // NSA_CMP_STRIDE
    width = NSA_CMP_STRIDE * LANES
    xk = nkc.reshape(b, rows, width)
    xv = nvc.reshape(b, rows, width)
    pk, wak, wbk, w2k = _compress_weights(pe_k, w1_k, w2_k)
    pv, wav, wbv, w2v = _compress_weights(pe_v, w1_v, w2_v)
    hid2 = wak.shape[1]
    xspec = pl.BlockSpec((1, rows, width), lambda i: (i, 0, 0))
    full = lambda a: pl.BlockSpec(a.shape, lambda i: (0,) * a.ndim)
    ospec = pl.BlockSpec((1, rows, LANES), lambda i: (i, 0, 0))
    return pl.pallas_call(
        _compress_kernel,
        grid=(b,),
        in_specs=[xspec, xspec, full(pk), full(pv), full(wak), full(wbk), full(w2k),
                  full(wav), full(wbv), full(w2v)],
        out_specs=[ospec, ospec],
        out_shape=[jax.ShapeDtypeStruct((b, rows, LANES), BF16)] * 2,
        compiler_params=_params("arbitrary"),
        name="nsa_compress",
    )(xk, xv, pk, pv, wak, wbk, w2k, wav, wbv, w2v)


def _flash_step(problems, carry=None, bias=None):
    biases = bias if isinstance(bias, (list, tuple)) else [bias] * len(problems)
    chains = []
    for (q_aug, k_aug, v_aug), b in zip(problems, biases):
        for i in range(q_aug.shape[0] // FLASH_ROWS):
            chains.append((i, _nt(q_aug[i * FLASH_ROWS:(i + 1) * FLASH_ROWS], k_aug), v_aug, b))
    out = []
    for c, (i, s, v_aug, bias) in enumerate(chains):
        if bias is not None:
            r0 = (i * FLASH_ROWS) % bias.shape[0]
            bi = bias[r0:r0 + FLASH_ROWS]
            nb = bi.shape[1]
            s = s + bi if nb == s.shape[1] else jnp.concatenate([s[:, :nb] + bi, s[:, nb:]], axis=1)
        m = jnp.max(s, axis=1, keepdims=True)
        if carry is None:
            acc = _mm(jnp.exp(s - m).astype(BF16), v_aug)
        else:
            m_old = carry[2 * c]
            m = jnp.maximum(m_old, m)
            acc = jnp.exp(m_old - m) * carry[2 * c + 1] + _mm(jnp.exp(s - m).astype(BF16), v_aug)
        out.extend((m, acc))
    return tuple(out)


def _flash_finish(carry):
    return jnp.concatenate([acc[:, :LANES] / acc[:, LANES:] for acc in carry[1::2]], axis=0)


def _moba_kernel(q_ref, k_ref, v_ref, oh_ref, o_ref, kmean_ref):
    blk = q_ref.shape[1]
    npair = q_ref.shape[2] // LANES
    nb = k_ref.shape[1] // blk
    grp = MOBA_GROUP
    qt = pl.program_id(2)

    @pl.when(qt == 0)
    def _():
        for c in range(npair):
            for j in range(nb):
                kj = k_ref[0, j * blk:(j + 1) * blk, LANES * c:LANES * (c + 1)].astype(F32)
                kmean_ref[c, j:j + 1, :] = jnp.sum(kj, axis=0, keepdims=True) * (1.0 / blk)

    lane = _iota((blk, LANES), 1)
    lo = lane < HALF
    causal = jnp.where(_iota((blk, blk), 1) <= _iota((blk, blk), 0), 0.0, NEG)
    ones = jnp.ones((grp * blk, LANES), BF16)
    zero = jnp.zeros((blk, LANES), BF16)

    q_aug = []
    for c in range(npair):
        q = q_ref[0, :, LANES * c:LANES * (c + 1)]
        q2 = jnp.concatenate([jnp.where(lo, q, zero), jnp.where(lo, zero, q)], axis=0)
        km = kmean_ref[c]
        km_hi = km.astype(BF16)
        gt = _nt(km_hi, q2) + _nt((km - km_hi.astype(F32)).astype(BF16), q2)
        jidx = _iota(gt.shape, 0)
        cnt = jnp.zeros(gt.shape, F32)
        for i in range(nb):
            gi = gt[i:i + 1, :]
            beats = (gi > gt) | ((gi == gt) & (jidx > i))
            past_i = jnp.full(gt.shape, i, jnp.int32) < qt
            cnt = cnt + jnp.where(beats & past_i, 1.0, 0.0)
        keep = ((jidx < qt) & (cnt < float(MOBA_TOPK))) | (jidx == qt)
        bias_t = jnp.concatenate([jnp.where(keep, 0.0, NEG_BIAS),
                                  jnp.zeros((LANES - nb - 1, 2 * blk), F32),
                                  jnp.full((1, 2 * blk), NEG_BIAS, F32)], axis=0)
        q_aug.append(jnp.concatenate([q2 * jnp.asarray(HALF ** -0.5, BF16),
                                      bias_t.T.astype(BF16)], axis=1))

    def group_kv(c, gi):
        ks, vs = [], []
        for t in range(grp):
            j = qt - (gi * grp + t)
            sk = pl.multiple_of(jnp.maximum(j, 0) * blk, blk)
            so = pl.multiple_of(jnp.where(j >= 0, j, nb) * blk, blk)
            ks.append(jnp.concatenate([k_ref[0, pl.ds(sk, blk), LANES * c:LANES * (c + 1)],
                                       oh_ref[pl.ds(so, blk), :]], axis=1))
            vs.append(v_ref[0, pl.ds(sk, blk), LANES * c:LANES * (c + 1)])
        return jnp.concatenate(ks, axis=0), jnp.concatenate([jnp.concatenate(vs, axis=0), ones], axis=1)

    def run(n_groups):
        def branch():
            def problems(gi):
                return [(q_aug[c],) + group_kv(c, gi) for c in range(npair)]

            carry = _flash_step(problems(0), bias=causal)
            for gi in range(1, n_groups):
                carry = _flash_step(problems(gi), carry=carry)
            out = _flash_finish(carry)
            for c in range(npair):
                o_ref[0, :, LANES * c:LANES * (c + 1)] = jnp.where(
                    lo, out[2 * c * blk:(2 * c + 1) * blk],
                    out[(2 * c + 1) * blk:(2 * c + 2) * blk]).astype(o_ref.dtype)
        return branch

    lax.switch(qt // grp, [run(n) for n in range(1, nb // grp + 1)])


def _moba(mq, mk, mv, b, s):
    width = mq.shape[-1]
    blk = MOBA_BLOCK
    nb = s // blk
    pw = LANES * MOBA_PAIRS
    assert nb < LANES - 1 and nb % MOBA_GROUP == 0 and width % pw == 0 and blk % FLASH_ROWS == 0
    blk_id = jnp.where(jnp.arange(s + blk) < s, jnp.arange(s + blk) // blk, LANES - 1)
    oh = (blk_id[:, None] == jnp.arange(LANES)[None, :]).astype(BF16)
    qspec = pl.BlockSpec((1, blk, pw), lambda bi, hp, qt: (bi, qt, hp))
    kspec = pl.BlockSpec((1, s, pw), lambda bi, hp, qt: (bi, 0, hp))
    return pl.pallas_call(
        _moba_kernel,
        grid=(b, width // pw, nb),
        in_specs=[qspec, kspec, kspec, pl.BlockSpec((s + blk, LANES), lambda bi, hp, qt: (0, 0))],
        out_specs=qspec,
        out_shape=jax.ShapeDtypeStruct((b, s, width), BF16),
        scratch_shapes=[pltpu.VMEM((MOBA_PAIRS, nb, LANES), F32)],
        compiler_params=_params("arbitrary", "arbitrary", "arbitrary"),
        name="moba_attention",
    )(mq.reshape(b, s, width), mk.reshape(b, s, width), mv.reshape(b, s, width), oh)


def _nsa_kernel(q_ref, gate_ref, kc_ref, vc_ref, slw_ref, oh_ref, c2s_ref, o_ref):
    tq = q_ref.shape[1]
    n_cmp = kc_ref.shape[1]
    t0 = pl.program_id(1) * tq
    scale = jnp.asarray(HALF ** -0.5, BF16)

    lane = _iota((tq, LANES), 1)
    lo = lane < HALF
    qb = [q_ref[0, :, LANES * c:LANES * (c + 1)] * scale for c in range(4)]
    qb_sw = [_swap_halves(x) for x in qb]
    zero = jnp.zeros((tq, LANES), BF16)

    def group_queries(g):
        in_g = lo if g == 0 else jnp.logical_not(lo)
        parts = []
        for p in range(4):
            h = 4 * g + p
            x = qb[h // 2] if (h % 2) == g else qb_sw[h // 2]
            parts.append(jnp.where(in_g, x, zero))
        return jnp.concatenate(parts, axis=0), in_g

    qg = [group_queries(g) for g in range(2)]
    q_all = jnp.concatenate([qg[0][0], qg[1][0]], axis=0)

    def chain_rows(x):
        return jnp.concatenate([x] * max(1, FLASH_ROWS // tq), axis=0)

    def rel(n, mult=1):
        return mult * _iota((tq, n), 1) - _iota((tq, n), 0)

    def ones(n):
        return jnp.ones((n, LANES), BF16)

    cbias = chain_rows(jnp.where(rel(n_cmp, NSA_CMP_STRIDE) <= t0 - (NSA_CMP_LEN - 1), 0.0, NEG))
    vc_aug = jnp.concatenate([vc_ref[0], c2s_ref[...], ones(n_cmp)], axis=1)
    o_cmp, i8 = [], []
    cmp_scores = [_nt(q_all[i * FLASH_ROWS:(i + 1) * FLASH_ROWS], kc_ref[0])
                  for i in range(8 * tq // FLASH_ROWS)]
    for i, s in enumerate(cmp_scores):
        r0 = (i * FLASH_ROWS) % cbias.shape[0]
        s = s + cbias[r0:r0 + FLASH_ROWS]
        m = jnp.max(s, axis=1, keepdims=True)
        m = jnp.where(m <= 0.5 * NEG, 0.0, m)
        r = _mm(jnp.exp(s - m).astype(BF16), vc_aug)
        den = r[:, 2 * LANES:]
        inv = 1.0 / jnp.where(den > 0, den, 1.0)
        o_cmp.append(r[:, :LANES] * inv)
        i8.append(r[:, LANES:2 * LANES] * inv)
    o_cmp = jnp.concatenate(o_cmp, axis=0)
    i8 = jnp.concatenate(i8, axis=0)
    imp = jnp.where(lo, (i8[0:tq] + i8[tq:2 * tq]) + (i8[2 * tq:3 * tq] + i8[3 * tq:4 * tq]),
                    (i8[4 * tq:5 * tq] + i8[5 * tq:6 * tq]) + (i8[6 * tq:7 * tq] + i8[7 * tq:8 * tq]))

    w0 = pl.multiple_of(jnp.maximum(t0 - NSA_WINDOW, 0), tq)
    wspan = NSA_WINDOW + tq
    d = rel(wspan)
    wbias = jnp.where((d <= t0 - w0) & (d > t0 - w0 - NSA_WINDOW), 0.0, NEG)
    o_win = _flash_finish(_flash_step(
        [(q_all, slw_ref[0, pl.ds(w0, wspan), 2 * LANES:3 * LANES],
          jnp.concatenate([slw_ref[0, pl.ds(w0, wspan), 3 * LANES:4 * LANES], ones(wspan)], axis=1))],
        bias=chain_rows(wbias)))

    imp_t = imp.T
    nblk = NSA_SLC_BLOCK
    jrow = _iota((LANES, tq), 0) & (nblk - 1)
    own = (t0 + _iota((LANES, tq), 1)) >> (NSA_SLC_BLOCK.bit_length() - 1)
    forced = (jrow == 0) | (jrow == own) | (jrow == own - 1)
    valid = jrow <= own
    val = jnp.where(valid, jnp.where(forced, jnp.inf, imp_t), -jnp.inf)
    sub = 8
    jloc = _iota((sub, tq), 0)
    cnts = []
    for g in range(2):
        vg = val[nblk * g:nblk * (g + 1)]
        tiles = [vg[sub * r:sub * (r + 1)] for r in range(nblk // sub)]
        cnt_r = [jnp.zeros((sub, tq), F32) for _ in tiles]
        for i in range(nblk):
            vi = jnp.broadcast_to(vg[i:i + 1, :], (sub, tq))
            for r, vr in enumerate(tiles):
                if i < sub * r:
                    beats = vi >= vr
                elif i >= sub * (r + 1):
                    beats = vi > vr
                else:
                    beats = (vi > vr) | ((vi == vr) & (jloc > i - sub * r))
                cnt_r[r] = cnt_r[r] + jnp.where(beats, 1.0, 0.0)
        cnts.extend(cnt_r)
    cnt = jnp.concatenate(cnts, axis=0)
    keep = valid & (cnt < float(NSA_SLC_TOPK))
    bias_q = jnp.where(keep, 0.0, NEG_BIAS).T

    span = NSA_SPAN
    bias_g = [jnp.where(qg[g][1], bias_q, 0.0).astype(BF16) for g in range(2)]
    bias_all = jnp.concatenate([bias_g[0]] * 4 + [bias_g[1]] * 4, axis=0)
    q_aug = jnp.concatenate([q_all, bias_all], axis=1)
    gate = 1.0 / (1.0 + jnp.exp(-gate_ref[0]))
    last = pl.program_id(1) // (span // tq)

    def span_kv(j):
        rows = pl.ds(pl.multiple_of(j * span, span), span)
        k_aug = jnp.concatenate([slw_ref[0, rows, 0:LANES], oh_ref[rows, :]], axis=1)
        v_aug = jnp.concatenate([slw_ref[0, rows, LANES:2 * LANES], ones(span)], axis=1)
        return k_aug, v_aug

    def combine(o_slc):
        heads = []
        for h in range(8):
            r = slice(h * tq, (h + 1) * tq)
            heads.append(gate[:, 3 * h:3 * h + 1] * o_cmp[r]
                         + gate[:, 3 * h + 1:3 * h + 2] * o_slc[r]
                         + gate[:, 3 * h + 2:3 * h + 3] * o_win[r])
        for c in range(4):
            a_lo, a_hi = heads[2 * c], heads[2 * c + 1]
            if c // 2 == 0:
                a_hi = pltpu.roll(a_hi, HALF, 1)
            else:
                a_lo = pltpu.roll(a_lo, HALF, 1)
            o_ref[0, :, LANES * c:LANES * (c + 1)] = jnp.where(lo, a_lo, a_hi).astype(o_ref.dtype)

    sbias = jnp.where(rel(span) <= t0 - last * span, 0.0, NEG)
    carry = _flash_step([(q_aug,) + span_kv(last)], bias=chain_rows(sbias))
    carry = lax.fori_loop(0, last, lambda j, c: _flash_step([(q_aug,) + span_kv(j)], carry=c), carry)
    combine(_flash_finish(carry))


def _cmp_to_slc(n_cmp_pad, n_cmp, n_slc):
    rs = NSA_SLC_BLOCK // NSA_CMP_STRIDE
    rc = NSA_CMP_LEN // NSA_CMP_STRIDE
    j = np.arange(n_slc)[:, None, None]
    i = np.broadcast_to(rs * j + np.arange(rs)[None, :, None] - np.arange(rc)[None, None, :],
                        (n_slc, rs, rc))
    jj = np.broadcast_to(j, i.shape)
    ok = (i >= 0) & (i < n_cmp)
    mat = np.zeros((n_cmp, n_slc), np.float32)
    np.add.at(mat, (i[ok], jj[ok]), 1.0)
    out = np.zeros((n_cmp_pad, LANES), np.float32)
    for g in range(2):
        out[:n_cmp, HALF * g:HALF * g + n_slc] = mat
    return jnp.asarray(out, BF16)


def _nsa(nq, ngate, kc, vc, nslw, b, s):
    tq = NSA_TQ
    n_slc = s // NSA_SLC_BLOCK
    n_cmp = (s - NSA_CMP_LEN) // NSA_CMP_STRIDE + 1
    n_cmp_pad = kc.shape[1]
    assert n_slc <= HALF and s >= NSA_WINDOW + tq and s % NSA_SPAN == 0
    assert NSA_SLC_BLOCK & (NSA_SLC_BLOCK - 1) == 0 and (8 * tq) % FLASH_ROWS == 0
    width = nq.shape[-1]
    oh = ((jnp.arange(s)[:, None] // NSA_SLC_BLOCK)
          == (jnp.arange(LANES)[None, :] % HALF)).astype(BF16)
    c2s = _cmp_to_slc(n_cmp_pad, n_cmp, n_slc)
    tile = lambda w: pl.BlockSpec((1, tq, w), lambda bi, qt: (bi, qt, 0))
    perb = lambda r, w: pl.BlockSpec((1, r, w), lambda bi, qt: (bi, 0, 0))
    return pl.pallas_call(
        _nsa_kernel,
        grid=(b, s // tq),
        in_specs=[tile(width), tile(LANES), perb(n_cmp_pad, LANES), perb(n_cmp_pad, LANES),
                  perb(s, 4 * LANES), pl.BlockSpec((s, LANES), lambda bi, qt: (0, 0)),
                  pl.BlockSpec(c2s.shape, lambda bi, qt: (0, 0))],
        out_specs=tile(width),
        out_shape=jax.ShapeDtypeStruct((b, s, width), BF16),
        compiler_params=_params("arbitrary", "arbitrary"),
        name="nsa_attention",
    )(nq.reshape(b, s, width), ngate.reshape(b, s, LANES), kc, vc,
      nslw.reshape(b, s, 4 * LANES), oh, c2s)


def _ret_kernel(qk_ref, v_ref, cos_ref, sin_ref, dec_ref, rowdec_ref, cdec_ref, gn_ref, o_ref, r_ref):
    ct = qk_ref.shape[1]
    kw = RET_HEADS * RET_KEY_DIM

    @pl.when(pl.program_id(1) == 0)
    def _():
        r_ref[...] = jnp.zeros_like(r_ref)

    cc = dec_ref.shape[1]
    lane = _iota((cc, LANES), 1)
    lo = lane < HALF
    first = (lane & (RET_KEY_DIM - 1)) < RET_KEY_DIM // 2
    rowdec = rowdec_ref[...]
    r_state = [r_ref[c] for c in range(RET_HEADS // 2)]

    for sub in range(ct // cc):
        rows = slice(sub * cc, (sub + 1) * cc)
        cos = cos_ref[rows, :]
        sin = sin_ref[rows, :]

        def rotate(t):
            partner = jnp.where(first, pltpu.roll(t, LANES - RET_KEY_DIM // 2, 1),
                                pltpu.roll(t, RET_KEY_DIM // 2, 1))
            return t * cos + partner * sin

        for c in range(RET_HEADS // 2):
            qc = rotate(qk_ref[0, rows, LANES * c:LANES * (c + 1)])
            kc = rotate(qk_ref[0, rows, kw + LANES * c:kw + LANES * (c + 1)]) * (RET_KEY_DIM ** -0.5)
            kcb = kc.astype(BF16)
            r_old = r_state[c]
            r_new = r_old * cdec_ref[c]
            r_oldb = r_old.astype(BF16)
            for half in range(2):
                h = 2 * c + half
                in_h = lo if half == 0 else jnp.logical_not(lo)
                qh = jnp.where(in_h, qc, 0.0).astype(BF16)
                vh = v_ref[0, rows, RET_VAL_DIM * h:RET_VAL_DIM * (h + 1)]
                sc = (_nt(qh, kcb) * dec_ref[h]).astype(BF16)
                o = _mm(sc, vh) + _mm(qh, r_oldb) * rowdec[:, h:h + 1]
                kd = jnp.where(in_h, kc, 0.0) * rowdec[:, 4 + h:5 + h]
                r_new = r_new + _mm(kd.T.astype(BF16), vh)
                mu = jnp.mean(o, axis=-1, keepdims=True)
                d = o - mu
                var = jnp.mean(d * d, axis=-1, keepdims=True)
                o = d * lax.rsqrt(var + EPS) * gn_ref[:, RET_VAL_DIM * h:RET_VAL_DIM * (h + 1)]
                o_ref[0, rows, RET_VAL_DIM * h:RET_VAL_DIM * (h + 1)] = o.astype(o_ref.dtype)
            r_state[c] = r_new

    for c in range(RET_HEADS // 2):
        r_ref[c] = r_state[c]


def _ret_tables(s, ct):
    h, dk = RET_HEADS, RET_KEY_DIM
    gamma = 1.0 - 2.0 ** (-5.0 - np.arange(h))
    log_g = jnp.asarray(np.log(gamma).astype(np.float32))
    inv_freq = jnp.asarray((1.0 / (10000.0 ** np.linspace(0.0, 1.0, dk // 2))).astype(np.float32))
    ang = jnp.arange(s, dtype=F32)[:, None] * inv_freq[None, :]
    cos, sin = jnp.cos(ang), jnp.sin(ang)
    cos_t = jnp.tile(jnp.concatenate([cos, cos], axis=-1), (1, LANES // dk))
    sin_t = jnp.tile(jnp.concatenate([-sin, sin], axis=-1), (1, LANES // dk))
    idx = jnp.arange(ct, dtype=F32)
    diff = idx[:, None] - idx[None, :]
    intra = jnp.where(diff >= 0, jnp.exp(log_g[:, None, None] * jnp.maximum(diff, 0.0)), 0.0)
    cross = jnp.exp(log_g[:, None] * (idx[None, :] + 1.0))
    kdec = jnp.exp(log_g[:, None] * (ct - 1.0 - idx[None, :]))
    rowdec = jnp.zeros((ct, LANES), F32).at[:, 0:h].set(cross.T).at[:, h:2 * h].set(kdec.T)
    cd = jnp.exp(log_g * ct)
    cdec = jnp.broadcast_to(jnp.repeat(cd, HALF).reshape(h // 2, LANES, 1), (h // 2, LANES, LANES))
    return cos_t, sin_t, intra, rowdec, cdec


def _retention(rqk, rv, gn_g, b, s):
    ct = min(RET_TILE, s)
    cos_t, sin_t, intra, rowdec, cdec = _ret_tables(s, min(RET_CHUNK, ct))
    wqk = rqk.shape[-1]
    wv = rv.shape[-1]
    kw = wqk // 2
    const = lambda a: pl.BlockSpec(a.shape, lambda bi, ci: (0,) * a.ndim)
    return pl.pallas_call(
        _ret_kernel,
        grid=(b, s // ct),
        in_specs=[pl.BlockSpec((1, ct, wqk), lambda bi, ci: (bi, ci, 0)),
                  pl.BlockSpec((1, ct, wv), lambda bi, ci: (bi, ci, 0)),
                  pl.BlockSpec((ct, LANES), lambda bi, ci: (ci, 0)),
                  pl.BlockSpec((ct, LANES), lambda bi, ci: (ci, 0)),
                  const(intra), const(rowdec), const(cdec),
                  pl.BlockSpec((1, wv), lambda bi, ci: (0, 0))],
        out_specs=pl.BlockSpec((1, ct, wv), lambda bi, ci: (bi, ci, 0)),
        out_shape=jax.ShapeDtypeStruct((b, s, wv), BF16),
        scratch_shapes=[pltpu.VMEM((RET_HEADS // 2, LANES, LANES), F32)],
        compiler_params=_params("arbitrary", "arbitrary"),
        name="retention",
    )(rqk.reshape(b, s, wqk), rv.reshape(b, s, wv), cos_t, sin_t, intra, rowdec, cdec,
      gn_g.reshape(1, wv))


def _mem_kernel(q_ref, k_ref, v_ref, o_ref):
    nh = q_ref.shape[2] // LANES
    scale = jnp.asarray(LANES ** -0.5, F32)
    for h in range(nh):
        cs = slice(LANES * h, LANES * (h + 1))
        s = _nt(q_ref[0, :, cs], k_ref[0, :, cs]) * scale
        m = jnp.max(s, axis=1, keepdims=True)
        p = jnp.exp(s - m)
        p = (p / jnp.sum(p, axis=1, keepdims=True)).astype(BF16)
        o_ref[0, :, cs] = _mm(p, v_ref[0, :, cs]).astype(o_ref.dtype)


def _mem_attention(cq, mem_k, mem_v, b, s):
    width = cq.shape[-1]
    m = mem_k.shape[0] // b
    tq = min(MEM_TQ, s)
    return pl.pallas_call(
        _mem_kernel,
        grid=(b, s // tq),
        in_specs=[pl.BlockSpec((1, tq, width), lambda bi, qt: (bi, qt, 0)),
                  pl.BlockSpec((1, m, width), lambda bi, qt: (bi, 0, 0)),
                  pl.BlockSpec((1, m, width), lambda bi, qt: (bi, 0, 0))],
        out_specs=pl.BlockSpec((1, tq, width), lambda bi, qt: (bi, qt, 0)),
        out_shape=jax.ShapeDtypeStruct((b, s, width), BF16),
        compiler_params=_params("arbitrary", "arbitrary"),
        name="memory_attention",
    )(cq.reshape(b, s, width), mem_k.reshape(b, m, width), mem_v.reshape(b, m, width))


def _out_kernel(oa_ref, ob_ref, oc_ref, od_ref, z_ref, x_ref, w_ref, g_ref, y_ref):
    gw = oa_ref.shape[1]
    acc = None
    for i, o_ref in enumerate((oa_ref, ob_ref, oc_ref, od_ref)):
        z = z_ref[:, gw * i:gw * (i + 1)].astype(F32)
        gated = (o_ref[...].astype(F32) * (z * (1.0 / (1.0 + jnp.exp(-z))))).astype(BF16)
        part = _mm(gated, w_ref[gw * i:gw * (i + 1), :])
        acc = part if acc is None else acc + part
    ms = jnp.mean(acc * acc, axis=-1, keepdims=True)
    y_ref[...] = x_ref[...] + acc * lax.rsqrt(ms + EPS) * g_ref[...]


def _out_proj(oa, ob, oc, od, z, x2, w_out, post_g):
    n, d = x2.shape
    gw = oa.shape[-1]
    tm = min(ROW_TILE, n)
    rows = lambda w: pl.BlockSpec((tm, w), lambda i: (i, 0))
    return pl.pallas_call(
        _out_kernel,
        grid=(n // tm,),
        in_specs=[rows(gw), rows(gw), rows(gw), rows(gw), rows(4 * gw), rows(d),
                  pl.BlockSpec(w_out.shape, lambda i: (0, 0)),
                  pl.BlockSpec((1, d), lambda i: (0, 0))],
        out_specs=rows(d),
        out_shape=jax.ShapeDtypeStruct((n, d), F32),
        compiler_params=_params("arbitrary"),
        name="gate_out_proj",
    )(oa.reshape(n, gw), ob.reshape(n, gw), oc.reshape(n, gw), od.reshape(n, gw), z, x2,
      w_out, post_g.reshape(1, d))


def _pad_gate_cols(w_in, gate_off, gate_w):
    d = w_in.shape[0]
    return jnp.concatenate([w_in[:, :gate_off + gate_w],
                            jnp.zeros((d, LANES - gate_w), w_in.dtype),
                            w_in[:, gate_off + gate_w:]], axis=1)


def _layer(x2, mem2, b, s, pre_g, post_g, mem_g, w_in, w_mem_kv,
           pe_k, w1_k, w2_k, pe_v, w1_v, w2_v, ret_gn_g, w_out):
    gw = w_out.shape[0] // 4
    kvw = LANES
    gate_w = 3 * 8
    gate_off = 4 * gw + 6 * kvw
    w = _pad_gate_cols(w_in, gate_off, gate_w).astype(BF16)
    widths = (gw, gw, gw, gw, kvw, kvw, 4 * kvw, LANES, gw, gw, gw, 4 * gw)
    dtypes = (BF16, BF16, BF16, BF16, BF16, BF16, BF16, F32, F32, BF16, BF16, BF16)
    (mq, mk, mv, nq, nkc, nvc, nslw, ngate, rqk, rv, cq, z) = _norm_matmul(
        x2, pre_g, w, widths, dtypes, "norm_in_proj")
    mem_k, mem_v = _norm_matmul(mem2, mem_g, w_mem_kv.astype(BF16), (gw, gw), (BF16, BF16),
                                "norm_mem_kv")

    o_moba = _moba(mq, mk, mv, b, s)
    kc, vc = _compress(nkc, nvc, pe_k, w1_k, w2_k, pe_v, w1_v, w2_v, b, s)
    o_nsa = _nsa(nq, ngate, kc, vc, nslw, b, s)
    o_ret = _retention(rqk, rv, ret_gn_g, b, s)
    o_mem = _mem_attention(cq, mem_k, mem_v, b, s)
    return _out_proj(o_moba, o_nsa, o_ret, o_mem, z, x2, w_out.astype(BF16), post_g)


def kernel(x, mem, pre_norm_g, post_norm_g, mem_norm_g, w_in, w_mem_kv, nsa_pe_k, nsa_w1_k, nsa_w2_k,
           nsa_pe_v, nsa_w1_v, nsa_w2_v, ret_gn_g, w_out):
    b, s, d = x.shape
    x2 = x.reshape(b * s, d)
    mem2 = mem.reshape(b * mem.shape[1], d)
    for l in range(w_in.shape[0]):
        x2 = _layer(x2, mem2, b, s, pre_norm_g[l], post_norm_g[l], mem_norm_g[l], w_in[l], w_mem_kv[l],
                    nsa_pe_k[l], nsa_w1_k[l], nsa_w2_k[l], nsa_pe_v[l], nsa_w1_v[l], nsa_w2_v[l],
                    ret_gn_g[l], w_out[l])
    return x2.reshape(b, s, d)
```

```python
import functools

import numpy as np
import jax
import jax.numpy as jnp
from jax import lax
from jax.experimental import pallas as pl
from jax.experimental.pallas import tpu as pltpu

F32 = jnp.float32
BF16 = jnp.bfloat16

EPS = 1e-6
LANES = 128
HALF = 64
NEG = -float(2 ** 100)
NEG_BIAS = NEG
V7X_VMEM_BYTES = 64 * 1024 * 1024
VMEM_LIMIT = V7X_VMEM_BYTES * 7 // 8

FLASH_ROWS = 256
MOBA_BLOCK = 256
MOBA_TOPK = 3
MOBA_GROUP = 4
MOBA_PAIRS = 4
NSA_CMP_LEN = 32
NSA_CMP_STRIDE = 16
NSA_SLC_BLOCK = 64
NSA_SLC_TOPK = 16
NSA_WINDOW = 512
NSA_TQ = 256
NSA_SPAN = 1024
RET_HEADS = 4
RET_KEY_DIM = 64
RET_VAL_DIM = 128
RET_CHUNK = 256
RET_TILE = 1024
ROW_TILE = 512


def _nt(a, b):
    return lax.dot_general(a, b, (((1,), (1,)), ((), ())), preferred_element_type=F32)


def _mm(a, b):
    return jnp.dot(a, b, preferred_element_type=F32)


def _iota(shape, dim):
    return lax.broadcasted_iota(jnp.int32, shape, dim)


def _swap_halves(x):
    return pltpu.roll(x.astype(F32), HALF, 1).astype(x.dtype)


def _params(*sem):
    return pltpu.CompilerParams(dimension_semantics=sem, vmem_limit_bytes=VMEM_LIMIT)


def _norm_matmul_kernel(x_ref, g_ref, w_ref, *out_refs, widths):
    x = x_ref[...]
    ms = jnp.mean(x * x, axis=-1, keepdims=True)
    h = (x * lax.rsqrt(ms + EPS) * g_ref[...]).astype(BF16)
    off = 0
    for o_ref, wd in zip(out_refs, widths):
        o_ref[...] = _mm(h, w_ref[:, off:off + wd]).astype(o_ref.dtype)
        off += wd


def _norm_matmul(x2, g, w, widths, dtypes, name):
    n, d = x2.shape
    tm = min(ROW_TILE, n)
    nc = w.shape[1]
    assert sum(widths) == nc and n % tm == 0
    return pl.pallas_call(
        functools.partial(_norm_matmul_kernel, widths=tuple(widths)),
        grid=(n // tm,),
        in_specs=[pl.BlockSpec((tm, d), lambda i: (i, 0)),
                  pl.BlockSpec((1, d), lambda i: (0, 0)),
                  pl.BlockSpec((d, nc), lambda i: (0, 0))],
        out_specs=[pl.BlockSpec((tm, wd), lambda i: (i, 0)) for wd in widths],
        out_shape=[jax.ShapeDtypeStruct((n, wd), dt) for wd, dt in zip(widths, dtypes)],
        compiler_params=_params("arbitrary"),
        name=name,
    )(x2, g.reshape(1, d), w)


def _compress_kernel(xk_ref, xv_ref, pek_ref, pev_ref, wak_ref, wbk_ref, w2k_ref,
                     wav_ref, wbv_ref, w2v_ref, kc_ref, vc_ref):
    def one(x_ref, pe_ref, wa_ref, wb_ref, w2_ref, o_ref):
        x = x_ref[0].astype(F32)
        pe = pe_ref[...]
        u = _mm((x + pe[0:1, :]).astype(BF16), wa_ref[...])
        v = _mm((x + pe[1:2, :]).astype(BF16), wb_ref[...])
        n = u.shape[0]
        hid = u + pltpu.roll(v, n - 1, 0)
        hid = hid * (1.0 / (1.0 + jnp.exp(-hid)))
        o_ref[0] = _mm(hid.astype(BF16), w2_ref[...]).astype(o_ref.dtype)

    one(xk_ref, pek_ref, wak_ref, wbk_ref, w2k_ref, kc_ref)
    one(xv_ref, pev_ref, wav_ref, wbv_ref, w2v_ref, vc_ref)


def _compress_weights(pe, w1, w2):
    l, dh = pe.shape
    hidden = w1.shape[1]
    half = l // 2
    eye = jnp.eye(2, dtype=F32)
    w1r = w1.reshape(2, half, dh, hidden)
    wab = jnp.einsum('sodj,gh->sogdhj', w1r, eye).reshape(2, half * 2 * dh, 2 * hidden)
    w2b = jnp.einsum('jd,gh->gjhd', w2, eye).reshape(2 * hidden, 2 * dh)
    pe2 = jnp.broadcast_to(pe.reshape(2, half, 1, dh), (2, half, 2, dh)).reshape(2, half * 2 * dh)
    return pe2, wab[0].astype(BF16), wab[1].astype(BF16), w2b.astype(BF16)


def _compress(nkc, nvc, pe_k, w1_k, w2_k, pe_v, w1_v, w2_v, b, s):
    rows = s // NSA_CMP_STRIDE
    width = NSA_CMP_STRIDE * LANES
    xk = nkc.reshape(b, rows, width)
    xv = nvc.reshape(b, rows, width)
    pk, wak, wbk, w2k = _compress_weights(pe_k, w1_k, w2_k)
    pv, wav, wbv, w2v = _compress_weights(pe_v, w1_v, w2_v)
    hid2 = wak.shape[1]
    xspec = pl.BlockSpec((1, rows, width), lambda i: (i, 0, 0))
    full = lambda a: pl.BlockSpec(a.shape, lambda i: (0,) * a.ndim)
    ospec = pl.BlockSpec((1, rows, LANES), lambda i: (i, 0, 0))
    return pl.pallas_call(
        _compress_kernel,
        grid=(b,),
        in_specs=[xspec, xspec, full(pk), full(pv), full(wak), full(wbk), full(w2k),
                  full(wav), full(wbv), full(w2v)],
        out_specs=[ospec, ospec],
        out_shape=[jax.ShapeDtypeStruct((b, rows, LANES), BF16)] * 2,
        compiler_params=_params("arbitrary"),
        name="nsa_compress",
    )(xk, xv, pk, pv, wak, wbk, w2k, wav, wbv, w2v)


def _flash_step(problems, carry=None, bias=None):
    biases = bias if isinstance(bias, (list, tuple)) else [bias] * len(problems)
    chains = []
    for (q_aug, k_aug, v_aug), b in zip(problems, biases):
        for i in range(q_aug.shape[0] // FLASH_ROWS):
            chains.append((i, _nt(q_aug[i * FLASH_ROWS:(i + 1) * FLASH_ROWS], k_aug), v_aug, b))
    out = []
    for c, (i, s, v_aug, bias) in enumerate(chains):
        if bias is not None:
            r0 = (i * FLASH_ROWS) % bias.shape[0]
            bi = bias[r0:r0 + FLASH_ROWS]
            nb = bi.shape[1]
            s = s + bi if nb == s.shape[1] else jnp.concatenate([s[:, :nb] + bi, s[:, nb:]], axis=1)
        m = jnp.max(s, axis=1, keepdims=True)
        if carry is None:
            acc = _mm(jnp.exp(s - m).astype(BF16), v_aug)
        else:
            m_old = carry[2 * c]
            m = jnp.maximum(m_old, m)
            acc = jnp.exp(m_old - m) * carry[2 * c + 1] + _mm(jnp.exp(s - m).astype(BF16), v_aug)
        out.extend((m, acc))
    return tuple(out)


def _flash_finish(carry):
    return jnp.concatenate([acc[:, :LANES] / acc[:, LANES:] for acc in carry[1::2]], axis=0)


def _moba_kernel(q_ref, k_ref, v_ref, oh_ref, o_ref, kmean_ref):
    blk = q_ref.shape[1]
    npair = q_ref.shape[2] // LANES
    nb = k_ref.shape[1] // blk
    grp = MOBA_GROUP
    qt = pl.program_id(2)

    @pl.when(qt == 0)
    def _():
        for c in range(npair):
            for j in range(nb):
                kj = k_ref[0, j * blk:(j + 1) * blk, LANES * c:LANES * (c + 1)].astype(F32)
                kmean_ref[c, j:j + 1, :] = jnp.sum(kj, axis=0, keepdims=True) * (1.0 / blk)

    lane = _iota((blk, LANES), 1)
    lo = lane < HALF
    causal = jnp.where(_iota((blk, blk), 1) <= _iota((blk, blk), 0), 0.0, NEG)
    ones = jnp.ones((grp * blk, LANES), BF16)
    zero = jnp.zeros((blk, LANES), BF16)

    q_aug = []
    for c in range(npair):
        q = q_ref[0, :, LANES * c:LANES * (c + 1)]
        q2 = jnp.concatenate([jnp.where(lo, q, zero), jnp.where(lo, zero, q)], axis=0)
        km = kmean_ref[c]
        km_hi = km.astype(BF16)
        gt = _nt(km_hi, q2) + _nt((km - km_hi.astype(F32)).astype(BF16), q2)
        jidx = _iota(gt.shape, 0)
        cnt = jnp.zeros(gt.shape, F32)
        for i in range(nb):
            gi = gt[i:i + 1, :]
            beats = (gi > gt) | ((gi == gt) & (jidx > i))
            past_i = jnp.full(gt.shape, i, jnp.int32) < qt
            cnt = cnt + jnp.where(beats & past_i, 1.0, 0.0)
        keep = ((jidx < qt) & (cnt < float(MOBA_TOPK))) | (jidx == qt)
        bias_t = jnp.concatenate([jnp.where(keep, 0.0, NEG_BIAS),
                                  jnp.zeros((LANES - nb - 1, 2 * blk), F32),
                                  jnp.full((1, 2 * blk), NEG_BIAS, F32)], axis=0)
        q_aug.append(jnp.concatenate([q2 * jnp.asarray(HALF ** -0.5, BF16),
                                      bias_t.T.astype(BF16)], axis=1))

    def group_kv(c, gi):
        ks, vs = [], []
        for t in range(grp):
            j = qt - (gi * grp + t)
            sk = pl.multiple_of(jnp.maximum(j, 0) * blk, blk)
            so = pl.multiple_of(jnp.where(j >= 0, j, nb) * blk, blk)
            ks.append(jnp.concatenate([k_ref[0, pl.ds(sk, blk), LANES * c:LANES * (c + 1)],
                                       oh_ref[pl.ds(so, blk), :]], axis=1))
            vs.append(v_ref[0, pl.ds(sk, blk), LANES * c:LANES * (c + 1)])
        return jnp.concatenate(ks, axis=0), jnp.concatenate([jnp.concatenate(vs, axis=0), ones], axis=1)

    def run(n_groups):
        def branch():
            def problems(gi):
                return [(q_aug[c],) + group_kv(c, gi) for c in range(npair)]

            carry = _flash_step(problems(0), bias=causal)
            for gi in range(1, n_groups):
                carry = _flash_step(problems(gi), carry=carry)
            out = _flash_finish(carry)
            for c in range(npair):
                o_ref[0, :, LANES * c:LANES * (c + 1)] = jnp.where(
                    lo, out[2 * c * blk:(2 * c + 1) * blk],
                    out[(2 * c + 1) * blk:(2 * c + 2) * blk]).astype(o_ref.dtype)
        return branch

    lax.switch(qt // grp, [run(n) for n in range(1, nb // grp + 1)])


def _moba(mq, mk, mv, b, s):
    width = mq.shape[-1]
    blk = MOBA_BLOCK
    nb = s // blk
    pw = LANES * MOBA_PAIRS
    assert nb < LANES - 1 and nb % MOBA_GROUP == 0 and width % pw == 0 and blk % FLASH_ROWS == 0
    blk_id = jnp.where(jnp.arange(s + blk) < s, jnp.arange(s + blk) // blk, LANES - 1)
    oh = (blk_id[:, None] == jnp.arange(LANES)[None, :]).astype(BF16)
    qspec = pl.BlockSpec((1, blk, pw), lambda bi, hp, qt: (bi, qt, hp))
    kspec = pl.BlockSpec((1, s, pw), lambda bi, hp, qt: (bi, 0, hp))
    return pl.pallas_call(
        _moba_kernel,
        grid=(b, width // pw, nb),
        in_specs=[qspec, kspec, kspec, pl.BlockSpec((s + blk, LANES), lambda bi, hp, qt: (0, 0))],
        out_specs=qspec,
        out_shape=jax.ShapeDtypeStruct((b, s, width), BF16),
        scratch_shapes=[pltpu.VMEM((MOBA_PAIRS, nb, LANES), F32)],
        compiler_params=_params("arbitrary", "arbitrary", "arbitrary"),
        name="moba_attention",
    )(mq.reshape(b, s, width), mk.reshape(b, s, width), mv.reshape(b, s, width), oh)


def _nsa_kernel(q_ref, gate_ref, kc_ref, vc_ref, slw_ref, oh_ref, c2s_ref, o_ref):
    tq = q_ref.shape[1]
    n_cmp = kc_ref.shape[1]
    t0 = pl.program_id(1) * tq
    scale = jnp.asarray(HALF ** -0.5, BF16)

    lane = _iota((tq, LANES), 1)
    lo = lane < HALF
    qb = [q_ref[0, :, LANES * c:LANES * (c + 1)] * scale for c in range(4)]
    qb_sw = [_swap_halves(x) for x in qb]
    zero = jnp.zeros((tq, LANES), BF16)

    def group_queries(g):
        in_g = lo if g == 0 else jnp.logical_not(lo)
        parts = []
        for p in range(4):
            h = 4 * g + p
            x = qb[h // 2] if (h % 2) == g else qb_sw[h // 2]
            parts.append(jnp.where(in_g, x, zero))
        return jnp.concatenate(parts, axis=0), in_g

    qg = [group_queries(g) for g in range(2)]
    q_all = jnp.concatenate([qg[0][0], qg[1][0]], axis=0)

    def chain_rows(x):
        return jnp.concatenate([x] * max(1, FLASH_ROWS // tq), axis=0)

    def rel(n, mult=1):
        return mult * _iota((tq, n), 1) - _iota((tq, n), 0)

    def ones(n):
        return jnp.ones((n, LANES), BF16)

    cbias = chain_rows(jnp.where(rel(n_cmp, NSA_CMP_STRIDE) <= t0 - (NSA_CMP_LEN - 1), 0.0, NEG))
    vc_aug = jnp.concatenate([vc_ref[0], c2s_ref[...], ones(n_cmp)], axis=1)
    o_cmp, i8 = [], []
    cmp_scores = [_nt(q_all[i * FLASH_ROWS:(i + 1) * FLASH_ROWS], kc_ref[0])
                  for i in range(8 * tq // FLASH_ROWS)]
    for i, s in enumerate(cmp_scores):
        r0 = (i * FLASH_ROWS) % cbias.shape[0]
        s = s + cbias[r0:r0 + FLASH_ROWS]
        m = jnp.max(s, axis=1, keepdims=True)
        m = jnp.where(m <= 0.5 * NEG, 0.0, m)
        r = _mm(jnp.exp(s - m).astype(BF16), vc_aug)
        den = r[:, 2 * LANES:]
        inv = 1.0 / jnp.where(den > 0, den, 1.0)
        o_cmp.append(r[:, :LANES] * inv)
        i8.append(r[:, LANES:2 * LANES] * inv)
    o_cmp = jnp.concatenate(o_cmp, axis=0)
    i8 = jnp.concatenate(i8, axis=0)
    imp = jnp.where(lo, (i8[0:tq] + i8[tq:2 * tq]) + (i8[2 * tq:3 * tq] + i8[3 * tq:4 * tq]),
                    (i8[4 * tq:5 * tq] + i8[5 * tq:6 * tq]) + (i8[6 * tq:7 * tq] + i8[7 * tq:8 * tq]))

    w0 = pl.multiple_of(jnp.maximum(t0 - NSA_WINDOW, 0), tq)
    wspan = NSA_WINDOW + tq
    d = rel(wspan)
    wbias = jnp.where((d <= t0 - w0) & (d > t0 - w0 - NSA_WINDOW), 0.0, NEG)
    o_win = _flash_finish(_flash_step(
        [(q_all, slw_ref[0, pl.ds(w0, wspan), 2 * LANES:3 * LANES],
          jnp.concatenate([slw_ref[0, pl.ds(w0, wspan), 3 * LANES:4 * LANES], ones(wspan)], axis=1))],
        bias=chain_rows(wbias)))

    imp_t = imp.T
    nblk = NSA_SLC_BLOCK
    jrow = _iota((LANES, tq), 0) & (nblk - 1)
    own = (t0 + _iota((LANES, tq), 1)) >> (NSA_SLC_BLOCK.bit_length() - 1)
    forced = (jrow == 0) | (jrow == own) | (jrow == own - 1)
    valid = jrow <= own
    val = jnp.where(valid, jnp.where(forced, jnp.inf, imp_t), -jnp.inf)
    sub = 8
    jloc = _iota((sub, tq), 0)
    cnts = []
    for g in range(2):
        vg = val[nblk * g:nblk * (g + 1)]
        tiles = [vg[sub * r:sub * (r + 1)] for r in range(nblk // sub)]
        cnt_r = [jnp.zeros((sub, tq), F32) for _ in tiles]
        for i in range(nblk):
            vi = jnp.broadcast_to(vg[i:i + 1, :], (sub, tq))
            for r, vr in enumerate(tiles):
                if i < sub * r:
                    beats = vi >= vr
                elif i >= sub * (r + 1):
                    beats = vi > vr
                else:
                    beats = (vi > vr) | ((vi == vr) & (jloc > i - sub * r))
                cnt_r[r] = cnt_r[r] + jnp.where(beats, 1.0, 0.0)
        cnts.extend(cnt_r)
    cnt = jnp.concatenate(cnts, axis=0)
    keep = valid & (cnt < float(NSA_SLC_TOPK))
    bias_q = jnp.where(keep, 0.0, NEG_BIAS).T

    span = NSA_SPAN
    bias_g = [jnp.where(qg[g][1], bias_q, 0.0).astype(BF16) for g in range(2)]
    bias_all = jnp.concatenate([bias_g[0]] * 4 + [bias_g[1]] * 4, axis=0)
    q_aug = jnp.concatenate([q_all, bias_all], axis=1)
    gate = 1.0 / (1.0 + jnp.exp(-gate_ref[0]))
    last = pl.program_id(1) // (span // tq)

    def span_kv(j):
        rows = pl.ds(pl.multiple_of(j * span, span), span)
        k_aug = jnp.concatenate([slw_ref[0, rows, 0:LANES], oh_ref[rows, :]], axis=1)
        v_aug = jnp.concatenate([slw_ref[0, rows, LANES:2 * LANES], ones(span)], axis=1)
        return k_aug, v_aug

    def combine(o_slc):
        heads = []
        for h in range(8):
            r = slice(h * tq, (h + 1) * tq)
            heads.append(gate[:, 3 * h:3 * h + 1] * o_cmp[r]
                         + gate[:, 3 * h + 1:3 * h + 2] * o_slc[r]
                         + gate[:, 3 * h + 2:3 * h + 3] * o_win[r])
        for c in range(4):
            a_lo, a_hi = heads[2 * c], heads[2 * c + 1]
            if c // 2 == 0:
                a_hi = pltpu.roll(a_hi, HALF, 1)
            else:
                a_lo = pltpu.roll(a_lo, HALF, 1)
            o_ref[0, :, LANES * c:LANES * (c + 1)] = jnp.where(lo, a_lo, a_hi).astype(o_ref.dtype)

    sbias = jnp.where(rel(span) <= t0 - last * span, 0.0, NEG)
    carry = _flash_step([(q_aug,) + span_kv(last)], bias=chain_rows(sbias))
    carry = lax.fori_loop(0, last, lambda j, c: _flash_step([(q_aug,) + span_kv(j)], carry=c), carry)
    combine(_flash_finish(carry))


def _cmp_to_slc(n_cmp_pad, n_cmp, n_slc):
    rs = NSA_SLC_BLOCK // NSA_CMP_STRIDE
    rc = NSA_CMP_LEN // NSA_CMP_STRIDE
    j = np.arange(n_slc)[:, None, None]
    i = np.broadcast_to(rs * j + np.arange(rs)[None, :, None] - np.arange(rc)[None, None, :],
                        (n_slc, rs, rc))
    jj = np.broadcast_to(j, i.shape)
    ok = (i >= 0) & (i < n_cmp)
    mat = np.zeros((n_cmp, n_slc), np.float32)
    np.add.at(mat, (i[ok], jj[ok]), 1.0)
    out = np.zeros((n_cmp_pad, LANES), np.float32)
    for g in range(2):
        out[:n_cmp, HALF * g:HALF * g + n_slc] = mat
    return jnp.asarray(out, BF16)


def _nsa(nq, ngate, kc, vc, nslw, b, s):
    tq = NSA_TQ
    n_slc = s // NSA_SLC_BLOCK
    n_cmp = (s - NSA_CMP_LEN) // NSA_CMP_STRIDE + 1
    n_cmp_pad = kc.shape[1]
    assert n_slc <= HALF and s >= NSA_WINDOW + tq and s % NSA_SPAN == 0
    assert NSA_SLC_BLOCK & (NSA_SLC_BLOCK - 1) == 0 and (8 * tq) % FLASH_ROWS == 0
    width = nq.shape[-1]
    oh = ((jnp.arange(s)[:, None] // NSA_SLC_BLOCK)
          == (jnp.arange(LANES)[None, :] % HALF)).astype(BF16)
    c2s = _cmp_to_slc(n_cmp_pad, n_cmp, n_slc)
    tile = lambda w: pl.BlockSpec((1, tq, w), lambda bi, qt: (bi, qt, 0))
    perb = lambda r, w: pl.BlockSpec((1, r, w), lambda bi, qt: (bi, 0, 0))
    return pl.pallas_call(
        _nsa_kernel,
        grid=(b, s // tq),
        in_specs=[tile(width), tile(LANES), perb(n_cmp_pad, LANES), perb(n_cmp_pad, LANES),
                  perb(s, 4 * LANES), pl.BlockSpec((s, LANES), lambda bi, qt: (0, 0)),
                  pl.BlockSpec(c2s.shape, lambda bi, qt: (0, 0))],
        out_specs=tile(width),
        out_shape=jax.ShapeDtypeStruct((b, s, width), BF16),
        compiler_params=_params("arbitrary", "arbitrary"),
        name="nsa_attention",
    )(nq.reshape(b, s, width), ngate.reshape(b, s, LANES), kc, vc,
      nslw.reshape(b, s, 4 * LANES), oh, c2s)


def _ret_kernel(qk_ref, v_ref, cos_ref, sin_ref, dec_ref, rowdec_ref, cdec_ref, gn_ref, o_ref, r_ref):
    ct = qk_ref.shape[1]
    kw = RET_HEADS * RET_KEY_DIM

    @pl.when(pl.program_id(1) == 0)
    def _():
        r_ref[...] = jnp.zeros_like(r_ref)

    cc = dec_ref.shape[1]
    lane = _iota((cc, LANES), 1)
    lo = lane < HALF
    first = (lane & (RET_KEY_DIM - 1)) < RET_KEY_DIM // 2
    rowdec = rowdec_ref[...]
    r_state = [r_ref[c] for c in range(RET_HEADS // 2)]

    for sub in range(ct // cc):
        rows = slice(sub * cc, (sub + 1) * cc)
        cos = cos_ref[rows, :]
        sin = sin_ref[rows, :]

        def rotate(t):
            partner = jnp.where(first, pltpu.roll(t, LANES - RET_KEY_DIM // 2, 1),
                                pltpu.roll(t, RET_KEY_DIM // 2, 1))
            return t * cos + partner * sin

        for c in range(RET_HEADS // 2):
            qc = rotate(qk_ref[0, rows, LANES * c:LANES * (c + 1)])
            kc = rotate(qk_ref[0, rows, kw + LANES * c:kw + LANES * (c + 1)]) * (RET_KEY_DIM ** -0.5)
            kcb = kc.astype(BF16)
            r_old = r_state[c]
            r_new = r_old * cdec_ref[c]
            r_oldb = r_old.astype(BF16)
            for half in range(2):
                h = 2 * c + half
                in_h = lo if half == 0 else jnp.logical_not(lo)
                qh = jnp.where(in_h, qc, 0.0).astype(BF16)
                vh = v_ref[0, rows, RET_VAL_DIM * h:RET_VAL_DIM * (h + 1)]
                sc = (_nt(qh, kcb) * dec_ref[h]).astype(BF16)
                o = _mm(sc, vh) + _mm(qh, r_oldb) * rowdec[:, h:h + 1]
                kd = jnp.where(in_h, kc, 0.0) * rowdec[:, 4 + h:5 + h]
                r_new = r_new + _mm(kd.T.astype(BF16), vh)
                mu = jnp.mean(o, axis=-1, keepdims=True)
                d = o - mu
                var = jnp.mean(d * d, axis=-1, keepdims=True)
                o = d * lax.rsqrt(var + EPS) * gn_ref[:, RET_VAL_DIM * h:RET_VAL_DIM * (h + 1)]
                o_ref[0, rows, RET_VAL_DIM * h:RET_VAL_DIM * (h + 1)] = o.astype(o_ref.dtype)
            r_state[c] = r_new

    for c in range(RET_HEADS // 2):
        r_ref[c] = r_state[c]


def _ret_tables(s, ct):
    h, dk = RET_HEADS, RET_KEY_DIM
    gamma = 1.0 - 2.0 ** (-5.0 - np.arange(h))
    log_g = jnp.asarray(np.log(gamma).astype(np.float32))
    inv_freq = jnp.asarray((1.0 / (10000.0 ** np.linspace(0.0, 1.0, dk // 2))).astype(np.float32))
    ang = jnp.arange(s, dtype=F32)[:, None] * inv_freq[None, :]
    cos, sin = jnp.cos(ang), jnp.sin(ang)
    cos_t = jnp.tile(jnp.concatenate([cos, cos], axis=-1), (1, LANES // dk))
    sin_t = jnp.tile(jnp.concatenate([-sin, sin], axis=-1), (1, LANES // dk))
    idx = jnp.arange(ct, dtype=F32)
    diff = idx[:, None] - idx[None, :]
    intra = jnp.where(diff >= 0, jnp.exp(log_g[:, None, None] * jnp.maximum(diff, 0.0)), 0.0)
    cross = jnp.exp(log_g[:, None] * (idx[None, :] + 1.0))
    kdec = jnp.exp(log_g[:, None] * (ct - 1.0 - idx[None, :]))
    rowdec = jnp.zeros((ct, LANES), F32).at[:, 0:h].set(cross.T).at[:, h:2 * h].set(kdec.T)
    cd = jnp.exp(log_g * ct)
    cdec = jnp.broadcast_to(jnp.repeat(cd, HALF).reshape(h // 2, LANES, 1), (h // 2, LANES, LANES))
    return cos_t, sin_t, intra, rowdec, cdec


def _retention(rqk, rv, gn_g, b, s):
    ct = min(RET_TILE, s)
    cos_t, sin_t, intra, rowdec, cdec = _ret_tables(s, min(RET_CHUNK, ct))
    wqk = rqk.shape[-1]
    wv = rv.shape[-1]
    kw = wqk // 2
    const = lambda a: pl.BlockSpec(a.shape, lambda bi, ci: (0,) * a.ndim)
    return pl.pallas_call(
        _ret_kernel,
        grid=(b, s // ct),
        in_specs=[pl.BlockSpec((1, ct, wqk), lambda bi, ci: (bi, ci, 0)),
                  pl.BlockSpec((1, ct, wv), lambda bi, ci: (bi, ci, 0)),
                  pl.BlockSpec((ct, LANES), lambda bi, ci: (ci, 0)),
                  pl.BlockSpec((ct, LANES), lambda bi, ci: (ci, 0)),
                  const(intra), const(rowdec), const(cdec),
                  pl.BlockSpec((1, wv), lambda bi, ci: (0, 0))],
        out_specs=pl.BlockSpec((1, ct, wv), lambda bi, ci: (bi, ci, 0)),
        out_shape=jax.ShapeDtypeStruct((b, s, wv), BF16),
        scratch_shapes=[pltpu.VMEM((RET_HEADS // 2, LANES, LANES), F32)],
        compiler_params=_params("arbitrary", "arbitrary"),
        name="retention",
    )(rqk.reshape(b, s, wqk), rv.reshape(b, s, wv), cos_t, sin_t, intra, rowdec, cdec,
      gn_g.reshape(1, wv))


def _out_kernel(oa_ref, ob_ref, oc_ref, cq_ref, mk_ref, mv_ref, z_ref, x_ref, w_ref, g_ref, y_ref):
    gw = oa_ref.shape[1]

    scale = jnp.asarray(LANES ** -0.5, F32)
    o_mem = []
    for h in range(gw // LANES):
        cs = slice(LANES * h, LANES * (h + 1))
        s = _nt(cq_ref[:, cs], mk_ref[0, :, cs]) * scale
        m = jnp.max(s, axis=1, keepdims=True)
        p = jnp.exp(s - m)
        p = (p / jnp.sum(p, axis=1, keepdims=True)).astype(BF16)
        o_mem.append(_mm(p, mv_ref[0, :, cs]))
    groups = [oa_ref[...].astype(F32), ob_ref[...].astype(F32), oc_ref[...].astype(F32),
              jnp.concatenate(o_mem, axis=1)]

    acc = None
    for i, o in enumerate(groups):
        z = z_ref[:, gw * i:gw * (i + 1)].astype(F32)
        gated = (o * (z * (1.0 / (1.0 + jnp.exp(-z))))).astype(BF16)
        part = _mm(gated, w_ref[gw * i:gw * (i + 1), :])
        acc = part if acc is None else acc + part
    ms = jnp.mean(acc * acc, axis=-1, keepdims=True)
    y_ref[...] = x_ref[...] + acc * lax.rsqrt(ms + EPS) * g_ref[...]


def _out_proj(oa, ob, oc, cq, mem_k, mem_v, z, x2, w_out, post_g, b):
    n, d = x2.shape
    gw = oa.shape[-1]
    tm = min(ROW_TILE, n // b)
    steps_per_b = n // b // tm
    m = mem_k.shape[0] // b
    rows = lambda w: pl.BlockSpec((tm, w), lambda i: (i, 0))
    mem = pl.BlockSpec((1, m, gw), lambda i: (i // steps_per_b, 0, 0))
    return pl.pallas_call(
        _out_kernel,
        grid=(n // tm,),
        in_specs=[rows(gw), rows(gw), rows(gw), rows(gw), mem, mem, rows(4 * gw), rows(d),
                  pl.BlockSpec(w_out.shape, lambda i: (0, 0)),
                  pl.BlockSpec((1, d), lambda i: (0, 0))],
        out_specs=rows(d),
        out_shape=jax.ShapeDtypeStruct((n, d), F32),
        compiler_params=_params("arbitrary"),
        name="mem_attn_gate_out_proj",
    )(oa.reshape(n, gw), ob.reshape(n, gw), oc.reshape(n, gw), cq, mem_k.reshape(b, m, gw),
      mem_v.reshape(b, m, gw), z, x2, w_out, post_g.reshape(1, d))


def _pad_gate_cols(w_in, gate_off, gate_w):
    d = w_in.shape[0]
    return jnp.concatenate([w_in[:, :gate_off + gate_w],
                            jnp.zeros((d, LANES - gate_w), w_in.dtype),
                            w_in[:, gate_off + gate_w:]], axis=1)


def _layer(x2, mem2, b, s, pre_g, post_g, mem_g, w_in, w_mem_kv,
           pe_k, w1_k, w2_k, pe_v, w1_v, w2_v, ret_gn_g, w_out):
    gw = w_out.shape[0] // 4
    kvw = LANES
    gate_w = 3 * 8
    gate_off = 4 * gw + 6 * kvw
    w = _pad_gate_cols(w_in, gate_off, gate_w).astype(BF16)
    widths = (gw, gw, gw, gw, kvw, kvw, 4 * kvw, LANES, gw, gw, gw, 4 * gw)
    dtypes = (BF16, BF16, BF16, BF16, BF16, BF16, BF16, F32, F32, BF16, BF16, BF16)
    (mq, mk, mv, nq, nkc, nvc, nslw, ngate, rqk, rv, cq, z) = _norm_matmul(
        x2, pre_g, w, widths, dtypes, "norm_in_proj")
    mem_k, mem_v = _norm_matmul(mem2, mem_g, w_mem_kv.astype(BF16), (gw, gw), (BF16, BF16),
                                "norm_mem_kv")

    o_moba = _moba(mq, mk, mv, b, s)
    kc, vc = _compress(nkc, nvc, pe_k, w1_k, w2_k, pe_v, w1_v, w2_v, b, s)
    o_nsa = _nsa(nq, ngate, kc, vc, nslw, b, s)
    o_ret = _retention(rqk, rv, ret_gn_g, b, s)
    return _out_proj(o_moba, o_nsa, o_ret, cq, mem_k, mem_v, z, x2, w_out.astype(BF16), post_g, b)


def kernel(x, mem, pre_norm_g, post_norm_g, mem_norm_g, w_in, w_mem_kv, nsa_pe_k, nsa_w1_k, nsa_w2_k,
           nsa_pe_v, nsa_w1_v, nsa_w2_v, ret_gn_g, w_out):
    b, s, d = x.shape
    x2 = x.reshape(b * s, d)
    mem2 = mem.reshape(b * mem.shape[1], d)
    for l in range(w_in.shape[0]):
        x2 = _layer(x2, mem2, b, s, pre_norm_g[l], post_norm_g[l], mem_norm_g[l], w_in[l], w_mem_kv[l],
                    nsa_pe_k[l], nsa_w1_k[l], nsa_w2_k[l], nsa_pe_v[l], nsa_w1_v[l], nsa_w2_v[l],
                    ret_gn_g[l], w_out[l])
    return x2.reshape(b, s, d)
```

```python
import functools

import numpy as np
import jax
import jax.numpy as jnp
from jax import lax
from jax.experimental import pallas as pl
from jax.experimental.pallas import tpu as pltpu

F32 = jnp.float32
BF16 = jnp.bfloat16

EPS = 1e-6
LANES = 128
HALF = 64
SCORE_SCALE = float(HALF ** -0.5 * np.log2(np.e))
NEG = -float(2 ** 100)
NEG_BIAS = NEG
V7X_VMEM_BYTES = 64 * 1024 * 1024
VMEM_LIMIT = V7X_VMEM_BYTES * 7 // 8

FLASH_ROWS = 256
MOBA_BLOCK = 256
MOBA_TOPK = 3
MOBA_GROUP = 4
MOBA_PAIRS = 4
NSA_CMP_LEN = 32
NSA_CMP_STRIDE = 16
NSA_SLC_BLOCK = 64
NSA_SLC_TOPK = 16
NSA_WINDOW = 512
NSA_TQ = 256
NSA_SPAN = 1024
RET_HEADS = 4
RET_KEY_DIM = 64
RET_VAL_DIM = 128
RET_CHUNK = 256
RET_TILE = 1024
ROW_TILE = 512


def _nt(a, b):
    return lax.dot_general(a, b, (((1,), (1,)), ((), ())), preferred_element_type=F32)


def _mm(a, b):
    return jnp.dot(a, b, preferred_element_type=F32)


def _iota(shape, dim):
    return lax.broadcasted_iota(jnp.int32, shape, dim)


def _swap_halves(x):
    return pltpu.roll(x.astype(F32), HALF, 1).astype(x.dtype)


def _params(*sem):
    return pltpu.CompilerParams(dimension_semantics=sem, vmem_limit_bytes=VMEM_LIMIT)


def _norm_matmul_kernel(x_ref, g_ref, w_ref, *out_refs, widths):
    x = x_ref[...]
    ms = jnp.mean(x * x, axis=-1, keepdims=True)
    h = (x * lax.rsqrt(ms + EPS) * g_ref[...]).astype(BF16)
    off = 0
    for o_ref, wd in zip(out_refs, widths):
        o_ref[...] = _mm(h, w_ref[:, off:off + wd]).astype(o_ref.dtype)
        off += wd


def _norm_matmul(x2, g, w, widths, dtypes, name):
    n, d = x2.shape
    tm = min(ROW_TILE, n)
    nc = w.shape[1]
    assert sum(widths) == nc and n % tm == 0
    return pl.pallas_call(
        functools.partial(_norm_matmul_kernel, widths=tuple(widths)),
        grid=(n // tm,),
        in_specs=[pl.BlockSpec((tm, d), lambda i: (i, 0)),
                  pl.BlockSpec((1, d), lambda i: (0, 0)),
                  pl.BlockSpec((d, nc), lambda i: (0, 0))],
        out_specs=[pl.BlockSpec((tm, wd), lambda i: (i, 0)) for wd in widths],
        out_shape=[jax.ShapeDtypeStruct((n, wd), dt) for wd, dt in zip(widths, dtypes)],
        compiler_params=_params("arbitrary"),
        name=name,
    )(x2, g.reshape(1, d), w)


def _compress_kernel(xk_ref, xv_ref, pek_ref, pev_ref, wak_ref, wbk_ref, w2k_ref,
                     wav_ref, wbv_ref, w2v_ref, kc_ref, vc_ref):
    def one(x_ref, pe_ref, wa_ref, wb_ref, w2_ref, o_ref):
        x = x_ref[0].astype(F32)
        pe = pe_ref[...]
        u = _mm((x + pe[0:1, :]).astype(BF16), wa_ref[...])
        v = _mm((x + pe[1:2, :]).astype(BF16), wb_ref[...])
        n = u.shape[0]
        hid = u + pltpu.roll(v, n - 1, 0)
        hid = hid * (1.0 / (1.0 + jnp.exp(-hid)))
        o_ref[0] = _mm(hid.astype(BF16), w2_ref[...]).astype(o_ref.dtype)

    one(xk_ref, pek_ref, wak_ref, wbk_ref, w2k_ref, kc_ref)
    one(xv_ref, pev_ref, wav_ref, wbv_ref, w2v_ref, vc_ref)


def _compress_weights(pe, w1, w2):
    l, dh = pe.shape
    hidden = w1.shape[1]
    half = l // 2
    eye = jnp.eye(2, dtype=F32)
    w1r = w1.reshape(2, half, dh, hidden)
    wab = jnp.einsum('sodj,gh->sogdhj', w1r, eye).reshape(2, half * 2 * dh, 2 * hidden)
    w2b = jnp.einsum('jd,gh->gjhd', w2, eye).reshape(2 * hidden, 2 * dh)
    pe2 = jnp.broadcast_to(pe.reshape(2, half, 1, dh), (2, half, 2, dh)).reshape(2, half * 2 * dh)
    return pe2, wab[0].astype(BF16), wab[1].astype(BF16), w2b.astype(BF16)


def _compress(nkc, nvc, pe_k, w1_k, w2_k, pe_v, w1_v, w2_v, b, s):
    rows = s // NSA_CMP_STRIDE
    width = NSA_CMP_STRIDE * LANES
    xk = nkc.reshape(b, rows, width)
    xv = nvc.reshape(b, rows, width)
    pk, wak, wbk, w2k = _compress_weights(pe_k, w1_k, w2_k)
    pv, wav, wbv, w2v = _compress_weights(pe_v, w1_v, w2_v)
    hid2 = wak.shape[1]
    xspec = pl.BlockSpec((1, rows, width), lambda i: (i, 0, 0))
    full = lambda a: pl.BlockSpec(a.shape, lambda i: (0,) * a.ndim)
    ospec = pl.BlockSpec((1, rows, LANES), lambda i: (i, 0, 0))
    return pl.pallas_call(
        _compress_kernel,
        grid=(b,),
        in_specs=[xspec, xspec, full(pk), full(pv), full(wak), full(wbk), full(w2k),
                  full(wav), full(wbv), full(w2v)],
        out_specs=[ospec, ospec],
        out_shape=[jax.ShapeDtypeStruct((b, rows, LANES), BF16)] * 2,
        compiler_params=_params("arbitrary"),
        name="nsa_compress",
    )(xk, xv, pk, pv, wak, wbk, w2k, wav, wbv, w2v)


def _flash_step(problems, carry=None, bias=None):
    biases = bias if isinstance(bias, (list, tuple)) else [bias] * len(problems)
    chains = []
    for (q_aug, k_aug, v_aug), b in zip(problems, biases):
        for i in range(q_aug.shape[0] // FLASH_ROWS):
            chains.append((i, _nt(q_aug[i * FLASH_ROWS:(i + 1) * FLASH_ROWS], k_aug), v_aug, b))
    out = []
    for c, (i, s, v_aug, bias) in enumerate(chains):
        if bias is not None:
            r0 = (i * FLASH_ROWS) % bias.shape[0]
            bi = bias[r0:r0 + FLASH_ROWS]
            nb = bi.shape[1]
            s = s + bi if nb == s.shape[1] else jnp.concatenate([s[:, :nb] + bi, s[:, nb:]], axis=1)
        m = jnp.max(s, axis=1, keepdims=True)
        if carry is None:
            acc = _mm(jnp.exp2(s - m).astype(BF16), v_aug)
        else:
            m_old = carry[2 * c]
            m = jnp.maximum(m_old, m)
            acc = jnp.exp2(m_old - m) * carry[2 * c + 1] + _mm(jnp.exp2(s - m).astype(BF16), v_aug)
        out.extend((m, acc))
    return tuple(out)


def _flash_finish(carry):
    return jnp.concatenate([acc[:, :LANES] / acc[:, LANES:] for acc in carry[1::2]], axis=0)


def _moba_kernel(q_ref, k_ref, v_ref, oh_ref, o_ref, kmean_ref):
    blk = q_ref.shape[1]
    npair = q_ref.shape[2] // LANES
    nb = k_ref.shape[1] // blk
    grp = MOBA_GROUP
    qt = pl.program_id(2)

    @pl.when(qt == 0)
    def _():
        for c in range(npair):
            for j in range(nb):
                kj = k_ref[0, j * blk:(j + 1) * blk, LANES * c:LANES * (c + 1)].astype(F32)
                kmean_ref[c, j:j + 1, :] = jnp.sum(kj, axis=0, keepdims=True) * (1.0 / blk)

    lane = _iota((blk, LANES), 1)
    lo = lane < HALF
    causal = jnp.where(_iota((blk, blk), 1) <= _iota((blk, blk), 0), 0.0, NEG)
    ones = jnp.ones((grp * blk, LANES), BF16)
    zero = jnp.zeros((blk, LANES), BF16)

    q_aug = []
    for c in range(npair):
        q = q_ref[0, :, LANES * c:LANES * (c + 1)]
        q2 = jnp.concatenate([jnp.where(lo, q, zero), jnp.where(lo, zero, q)], axis=0)
        km = kmean_ref[c]
        km_hi = km.astype(BF16)
        gt = _nt(km_hi, q2) + _nt((km - km_hi.astype(F32)).astype(BF16), q2)
        jidx = _iota(gt.shape, 0)
        cnt = jnp.zeros(gt.shape, F32)
        for i in range(nb):
            gi = gt[i:i + 1, :]
            beats = (gi > gt) | ((gi == gt) & (jidx > i))
            past_i = jnp.full(gt.shape, i, jnp.int32) < qt
            cnt = cnt + jnp.where(beats & past_i, 1.0, 0.0)
        keep = ((jidx < qt) & (cnt < float(MOBA_TOPK))) | (jidx == qt)
        bias_t = jnp.concatenate([jnp.where(keep, 0.0, NEG_BIAS),
                                  jnp.zeros((LANES - nb - 1, 2 * blk), F32),
                                  jnp.full((1, 2 * blk), NEG_BIAS, F32)], axis=0)
        q_aug.append(jnp.concatenate([(q2.astype(F32) * SCORE_SCALE).astype(BF16),
                                      bias_t.T.astype(BF16)], axis=1))

    def group_kv(c, gi):
        ks, vs = [], []
        for t in range(grp):
            j = qt - (gi * grp + t)
            sk = pl.multiple_of(jnp.maximum(j, 0) * blk, blk)
            so = pl.multiple_of(jnp.where(j >= 0, j, nb) * blk, blk)
            ks.append(jnp.concatenate([k_ref[0, pl.ds(sk, blk), LANES * c:LANES * (c + 1)],
                                       oh_ref[pl.ds(so, blk), :]], axis=1))
            vs.append(v_ref[0, pl.ds(sk, blk), LANES * c:LANES * (c + 1)])
        return jnp.concatenate(ks, axis=0), jnp.concatenate([jnp.concatenate(vs, axis=0), ones], axis=1)

    def run(n_groups):
        def branch():
            def problems(gi):
                return [(q_aug[c],) + group_kv(c, gi) for c in range(npair)]

            carry = _flash_step(problems(0), bias=causal)
            for gi in range(1, n_groups):
                carry = _flash_step(problems(gi), carry=carry)
            out = _flash_finish(carry)
            for c in range(npair):
                o_ref[0, :, LANES * c:LANES * (c + 1)] = jnp.where(
                    lo, out[2 * c * blk:(2 * c + 1) * blk],
                    out[(2 * c + 1) * blk:(2 * c + 2) * blk]).astype(o_ref.dtype)
        return branch

    lax.switch(qt // grp, [run(n) for n in range(1, nb // grp + 1)])


def _moba(mq, mk, mv, b, s):
    width = mq.shape[-1]
    blk = MOBA_BLOCK
    nb = s // blk
    pw = LANES * MOBA_PAIRS
    assert nb < LANES - 1 and nb % MOBA_GROUP == 0 and width % pw == 0 and blk % FLASH_ROWS == 0
    blk_id = jnp.where(jnp.arange(s + blk) < s, jnp.arange(s + blk) // blk, LANES - 1)
    oh = (blk_id[:, None] == jnp.arange(LANES)[None, :]).astype(BF16)
    qspec = pl.BlockSpec((1, blk, pw), lambda bi, hp, qt: (bi, qt, hp))
    kspec = pl.BlockSpec((1, s, pw), lambda bi, hp, qt: (bi, 0, hp))
    return pl.pallas_call(
        _moba_kernel,
        grid=(b, width // pw, nb),
        in_specs=[qspec, kspec, kspec, pl.BlockSpec((s + blk, LANES), lambda bi, hp, qt: (0, 0))],
        out_specs=qspec,
        out_shape=jax.ShapeDtypeStruct((b, s, width), BF16),
        scratch_shapes=[pltpu.VMEM((MOBA_PAIRS, nb, LANES), F32)],
        compiler_params=_params("arbitrary", "arbitrary", "arbitrary"),
        name="moba_attention",
    )(mq.reshape(b, s, width), mk.reshape(b, s, width), mv.reshape(b, s, width), oh)


def _nsa_kernel(q_ref, gate_ref, kc_ref, vc_ref, slw_ref, oh_ref, c2s_ref, o_ref):
    tq = q_ref.shape[1]
    n_cmp = kc_ref.shape[1]
    t0 = pl.program_id(1) * tq

    lane = _iota((tq, LANES), 1)
    lo = lane < HALF
    qb = [(q_ref[0, :, LANES * c:LANES * (c + 1)].astype(F32) * SCORE_SCALE).astype(BF16)
          for c in range(4)]
    qb_sw = [_swap_halves(x) for x in qb]
    zero = jnp.zeros((tq, LANES), BF16)

    def group_queries(g):
        in_g = lo if g == 0 else jnp.logical_not(lo)
        parts = []
        for p in range(4):
            h = 4 * g + p
            x = qb[h // 2] if (h % 2) == g else qb_sw[h // 2]
            parts.append(jnp.where(in_g, x, zero))
        return jnp.concatenate(parts, axis=0), in_g

    qg = [group_queries(g) for g in range(2)]
    q_all = jnp.concatenate([qg[0][0], qg[1][0]], axis=0)

    def chain_rows(x):
        return jnp.concatenate([x] * max(1, FLASH_ROWS // tq), axis=0)

    def rel(n, mult=1):
        return mult * _iota((tq, n), 1) - _iota((tq, n), 0)

    def ones(n):
        return jnp.ones((n, LANES), BF16)

    cbias = chain_rows(jnp.where(rel(n_cmp, NSA_CMP_STRIDE) <= t0 - (NSA_CMP_LEN - 1), 0.0, NEG))
    vc_aug = jnp.concatenate([vc_ref[0], c2s_ref[...], ones(n_cmp)], axis=1)
    o_cmp, i8 = [], []
    cmp_scores = [_nt(q_all[i * FLASH_ROWS:(i + 1) * FLASH_ROWS], kc_ref[0])
                  for i in range(8 * tq // FLASH_ROWS)]
    for i, s in enumerate(cmp_scores):
        r0 = (i * FLASH_ROWS) % cbias.shape[0]
        s = s + cbias[r0:r0 + FLASH_ROWS]
        m = jnp.max(s, axis=1, keepdims=True)
        m = jnp.where(m <= 0.5 * NEG, 0.0, m)
        r = _mm(jnp.exp2(s - m).astype(BF16), vc_aug)
        den = r[:, 2 * LANES:]
        inv = 1.0 / jnp.where(den > 0, den, 1.0)
        o_cmp.append(r[:, :LANES] * inv)
        i8.append(r[:, LANES:2 * LANES] * inv)
    o_cmp = jnp.concatenate(o_cmp, axis=0)
    i8 = jnp.concatenate(i8, axis=0)
    imp = jnp.where(lo, (i8[0:tq] + i8[tq:2 * tq]) + (i8[2 * tq:3 * tq] + i8[3 * tq:4 * tq]),
                    (i8[4 * tq:5 * tq] + i8[5 * tq:6 * tq]) + (i8[6 * tq:7 * tq] + i8[7 * tq:8 * tq]))

    w0 = pl.multiple_of(jnp.maximum(t0 - NSA_WINDOW, 0), tq)
    wspan = NSA_WINDOW + tq
    d = rel(wspan)
    wbias = jnp.where((d <= t0 - w0) & (d > t0 - w0 - NSA_WINDOW), 0.0, NEG)
    o_win = _flash_finish(_flash_step(
        [(q_all, slw_ref[0, pl.ds(w0, wspan), 2 * LANES:3 * LANES],
          jnp.concatenate([slw_ref[0, pl.ds(w0, wspan), 3 * LANES:4 * LANES], ones(wspan)], axis=1))],
        bias=chain_rows(wbias)))

    imp_t = imp.T
    nblk = NSA_SLC_BLOCK
    jrow = _iota((LANES, tq), 0) & (nblk - 1)
    own = (t0 + _iota((LANES, tq), 1)) >> (NSA_SLC_BLOCK.bit_length() - 1)
    forced = (jrow == 0) | (jrow == own) | (jrow == own - 1)
    valid = jrow <= own
    val = jnp.where(valid, jnp.where(forced, jnp.inf, imp_t), -jnp.inf)
    sub = 8
    jloc = _iota((sub, tq), 0)
    cnts = []
    for g in range(2):
        vg = val[nblk * g:nblk * (g + 1)]
        tiles = [vg[sub * r:sub * (r + 1)] for r in range(nblk // sub)]
        cnt_r = [jnp.zeros((sub, tq), F32) for _ in tiles]
        for i in range(nblk):
            vi = jnp.broadcast_to(vg[i:i + 1, :], (sub, tq))
            for r, vr in enumerate(tiles):
                if i < sub * r:
                    beats = vi >= vr
                elif i >= sub * (r + 1):
                    beats = vi > vr
                else:
                    beats = (vi > vr) | ((vi == vr) & (jloc > i - sub * r))
                cnt_r[r] = cnt_r[r] + jnp.where(beats, 1.0, 0.0)
        cnts.extend(cnt_r)
    cnt = jnp.concatenate(cnts, axis=0)
    keep = valid & (cnt < float(NSA_SLC_TOPK))
    bias_q = jnp.where(keep, 0.0, NEG_BIAS).T

    span = NSA_SPAN
    bias_g = [jnp.where(qg[g][1], bias_q, 0.0).astype(BF16) for g in range(2)]
    bias_all = jnp.concatenate([bias_g[0]] * 4 + [bias_g[1]] * 4, axis=0)
    q_aug = jnp.concatenate([q_all, bias_all], axis=1)
    gate = 1.0 / (1.0 + jnp.exp(-gate_ref[0]))
    last = pl.program_id(1) // (span // tq)

    def span_kv(j):
        rows = pl.ds(pl.multiple_of(j * span, span), span)
        k_aug = jnp.concatenate([slw_ref[0, rows, 0:LANES], oh_ref[rows, :]], axis=1)
        v_aug = jnp.concatenate([slw_ref[0, rows, LANES:2 * LANES], ones(span)], axis=1)
        return k_aug, v_aug

    def combine(o_slc):
        heads = []
        for h in range(8):
            r = slice(h * tq, (h + 1) * tq)
            heads.append(gate[:, 3 * h:3 * h + 1] * o_cmp[r]
                         + gate[:, 3 * h + 1:3 * h + 2] * o_slc[r]
                         + gate[:, 3 * h + 2:3 * h + 3] * o_win[r])
        for c in range(4):
            a_lo, a_hi = heads[2 * c], heads[2 * c + 1]
            if c // 2 == 0:
                a_hi = pltpu.roll(a_hi, HALF, 1)
            else:
                a_lo = pltpu.roll(a_lo, HALF, 1)
            o_ref[0, :, LANES * c:LANES * (c + 1)] = jnp.where(lo, a_lo, a_hi).astype(o_ref.dtype)

    sbias = jnp.where(rel(span) <= t0 - last * span, 0.0, NEG)
    carry = _flash_step([(q_aug,) + span_kv(last)], bias=chain_rows(sbias))
    carry = lax.fori_loop(0, last, lambda j, c: _flash_step([(q_aug,) + span_kv(j)], carry=c), carry)
    combine(_flash_finish(carry))


def _cmp_to_slc(n_cmp_pad, n_cmp, n_slc):
    rs = NSA_SLC_BLOCK // NSA_CMP_STRIDE
    rc = NSA_CMP_LEN // NSA_CMP_STRIDE
    j = np.arange(n_slc)[:, None, None]
    i = np.broadcast_to(rs * j + np.arange(rs)[None, :, None] - np.arange(rc)[None, None, :],
                        (n_slc, rs, rc))
    jj = np.broadcast_to(j, i.shape)
    ok = (i >= 0) & (i < n_cmp)
    mat = np.zeros((n_cmp, n_slc), np.float32)
    np.add.at(mat, (i[ok], jj[ok]), 1.0)
    out = np.zeros((n_cmp_pad, LANES), np.float32)
    for g in range(2):
        out[:n_cmp, HALF * g:HALF * g + n_slc] = mat
    return jnp.asarray(out, BF16)


def _nsa(nq, ngate, kc, vc, nslw, b, s):
    tq = NSA_TQ
    n_slc = s // NSA_SLC_BLOCK
    n_cmp = (s - NSA_CMP_LEN) // NSA_CMP_STRIDE + 1
    n_cmp_pad = kc.shape[1]
    assert n_slc <= HALF and s >= NSA_WINDOW + tq and s % NSA_SPAN == 0
    assert NSA_SLC_BLOCK & (NSA_SLC_BLOCK - 1) == 0 and (8 * tq) % FLASH_ROWS == 0
    width = nq.shape[-1]
    oh = ((jnp.arange(s)[:, None] // NSA_SLC_BLOCK)
          == (jnp.arange(LANES)[None, :] % HALF)).astype(BF16)
    c2s = _cmp_to_slc(n_cmp_pad, n_cmp, n_slc)
    tile = lambda w: pl.BlockSpec((1, tq, w), lambda bi, qt: (bi, qt, 0))
    perb = lambda r, w: pl.BlockSpec((1, r, w), lambda bi, qt: (bi, 0, 0))
    return pl.pallas_call(
        _nsa_kernel,
        grid=(b, s // tq),
        in_specs=[tile(width), tile(LANES), perb(n_cmp_pad, LANES), perb(n_cmp_pad, LANES),
                  perb(s, 4 * LANES), pl.BlockSpec((s, LANES), lambda bi, qt: (0, 0)),
                  pl.BlockSpec(c2s.shape, lambda bi, qt: (0, 0))],
        out_specs=tile(width),
        out_shape=jax.ShapeDtypeStruct((b, s, width), BF16),
        compiler_params=_params("arbitrary", "arbitrary"),
        name="nsa_attention",
    )(nq.reshape(b, s, width), ngate.reshape(b, s, LANES), kc, vc,
      nslw.reshape(b, s, 4 * LANES), oh, c2s)


def _ret_kernel(qk_ref, v_ref, cos_ref, sin_ref, dec_ref, rowdec_ref, cdec_ref, gn_ref, o_ref, r_ref):
    ct = qk_ref.shape[1]
    kw = RET_HEADS * RET_KEY_DIM

    @pl.when(pl.program_id(1) == 0)
    def _():
        r_ref[...] = jnp.zeros_like(r_ref)

    cc = dec_ref.shape[1]
    lane = _iota((cc, LANES), 1)
    lo = lane < HALF
    first = (lane & (RET_KEY_DIM - 1)) < RET_KEY_DIM // 2
    rowdec = rowdec_ref[...]
    r_state = [r_ref[c] for c in range(RET_HEADS // 2)]

    for sub in range(ct // cc):
        rows = slice(sub * cc, (sub + 1) * cc)
        cos = cos_ref[rows, :]
        sin = sin_ref[rows, :]

        def rotate(t):
            partner = jnp.where(first, pltpu.roll(t, LANES - RET_KEY_DIM // 2, 1),
                                pltpu.roll(t, RET_KEY_DIM // 2, 1))
            return t * cos + partner * sin

        for c in range(RET_HEADS // 2):
            qc = rotate(qk_ref[0, rows, LANES * c:LANES * (c + 1)])
            kc = rotate(qk_ref[0, rows, kw + LANES * c:kw + LANES * (c + 1)]) * (RET_KEY_DIM ** -0.5)
            kcb = kc.astype(BF16)
            r_old = r_state[c]
            r_new = r_old * cdec_ref[c]
            r_oldb = r_old.astype(BF16)
            for half in range(2):
                h = 2 * c + half
                in_h = lo if half == 0 else jnp.logical_not(lo)
                qh = jnp.where(in_h, qc, 0.0).astype(BF16)
                vh = v_ref[0, rows, RET_VAL_DIM * h:RET_VAL_DIM * (h + 1)]
                sc = (_nt(qh, kcb) * dec_ref[h]).astype(BF16)
                o = _mm(sc, vh) + _mm(qh, r_oldb) * rowdec[:, h:h + 1]
                kd = jnp.where(in_h, kc, 0.0) * rowdec[:, 4 + h:5 + h]
                r_new = r_new + _mm(kd.T.astype(BF16), vh)
                mu = jnp.mean(o, axis=-1, keepdims=True)
                d = o - mu
                var = jnp.mean(d * d, axis=-1, keepdims=True)
                o = d * lax.rsqrt(var + EPS) * gn_ref[:, RET_VAL_DIM * h:RET_VAL_DIM * (h + 1)]
                o_ref[0, rows, RET_VAL_DIM * h:RET_VAL_DIM * (h + 1)] = o.astype(o_ref.dtype)
            r_state[c] = r_new

    for c in range(RET_HEADS // 2):
        r_ref[c] = r_state[c]


def _ret_tables(s, ct):
    h, dk = RET_HEADS, RET_KEY_DIM
    gamma = 1.0 - 2.0 ** (-5.0 - np.arange(h))
    log_g = jnp.asarray(np.log(gamma).astype(np.float32))
    inv_freq = jnp.asarray((1.0 / (10000.0 ** np.linspace(0.0, 1.0, dk // 2))).astype(np.float32))
    ang = jnp.arange(s, dtype=F32)[:, None] * inv_freq[None, :]
    cos, sin = jnp.cos(ang), jnp.sin(ang)
    cos_t = jnp.tile(jnp.concatenate([cos, cos], axis=-1), (1, LANES // dk))
    sin_t = jnp.tile(jnp.concatenate([-sin, sin], axis=-1), (1, LANES // dk))
    idx = jnp.arange(ct, dtype=F32)
    diff = idx[:, None] - idx[None, :]
    intra = jnp.where(diff >= 0, jnp.exp(log_g[:, None, None] * jnp.maximum(diff, 0.0)), 0.0)
    cross = jnp.exp(log_g[:, None] * (idx[None, :] + 1.0))
    kdec = jnp.exp(log_g[:, None] * (ct - 1.0 - idx[None, :]))
    rowdec = jnp.zeros((ct, LANES), F32).at[:, 0:h].set(cross.T).at[:, h:2 * h].set(kdec.T)
    cd = jnp.exp(log_g * ct)
    cdec = jnp.broadcast_to(jnp.repeat(cd, HALF).reshape(h // 2, LANES, 1), (h // 2, LANES, LANES))
    return cos_t, sin_t, intra, rowdec, cdec


def _retention(rqk, rv, gn_g, b, s):
    ct = min(RET_TILE, s)
    cos_t, sin_t, intra, rowdec, cdec = _ret_tables(s, min(RET_CHUNK, ct))
    wqk = rqk.shape[-1]
    wv = rv.shape[-1]
    kw = wqk // 2
    const = lambda a: pl.BlockSpec(a.shape, lambda bi, ci: (0,) * a.ndim)
    return pl.pallas_call(
        _ret_kernel,
        grid=(b, s // ct),
        in_specs=[pl.BlockSpec((1, ct, wqk), lambda bi, ci: (bi, ci, 0)),
                  pl.BlockSpec((1, ct, wv), lambda bi, ci: (bi, ci, 0)),
                  pl.BlockSpec((ct, LANES), lambda bi, ci: (ci, 0)),
                  pl.BlockSpec((ct, LANES), lambda bi, ci: (ci, 0)),
                  const(intra), const(rowdec), const(cdec),
                  pl.BlockSpec((1, wv), lambda bi, ci: (0, 0))],
        out_specs=pl.BlockSpec((1, ct, wv), lambda bi, ci: (bi, ci, 0)),
        out_shape=jax.ShapeDtypeStruct((b, s, wv), BF16),
        scratch_shapes=[pltpu.VMEM((RET_HEADS // 2, LANES, LANES), F32)],
        compiler_params=_params("arbitrary", "arbitrary"),
        name="retention",
    )(rqk.reshape(b, s, wqk), rv.reshape(b, s, wv), cos_t, sin_t, intra, rowdec, cdec,
      gn_g.reshape(1, wv))


def _out_kernel(oa_ref, ob_ref, oc_ref, cq_ref, mk_ref, mv_ref, z_ref, x_ref, w_ref, g_ref, y_ref):
    gw = oa_ref.shape[1]

    scale = jnp.asarray(LANES ** -0.5, F32)
    o_mem = []
    for h in range(gw // LANES):
        cs = slice(LANES * h, LANES * (h + 1))
        s = _nt(cq_ref[:, cs], mk_ref[0, :, cs]) * scale
        m = jnp.max(s, axis=1, keepdims=True)
        p = jnp.exp(s - m)
        p = (p / jnp.sum(p, axis=1, keepdims=True)).astype(BF16)
        o_mem.append(_mm(p, mv_ref[0, :, cs]))
    groups = [oa_ref[...].astype(F32), ob_ref[...].astype(F32), oc_ref[...].astype(F32),
              jnp.concatenate(o_mem, axis=1)]

    acc = None
    for i, o in enumerate(groups):
        z = z_ref[:, gw * i:gw * (i + 1)].astype(F32)
        gated = (o * (z * (1.0 / (1.0 + jnp.exp(-z))))).astype(BF16)
        part = _mm(gated, w_ref[gw * i:gw * (i + 1), :])
        acc = part if acc is None else acc + part
    ms = jnp.mean(acc * acc, axis=-1, keepdims=True)
    y_ref[...] = x_ref[...] + acc * lax.rsqrt(ms + EPS) * g_ref[...]


def _out_proj(oa, ob, oc, cq, mem_k, mem_v, z, x2, w_out, post_g, b):
    n, d = x2.shape
    gw = oa.shape[-1]
    tm = min(ROW_TILE, n // b)
    steps_per_b = n // b // tm
    m = mem_k.shape[0] // b
    rows = lambda w: pl.BlockSpec((tm, w), lambda i: (i, 0))
    mem = pl.BlockSpec((1, m, gw), lambda i: (i // steps_per_b, 0, 0))
    return pl.pallas_call(
        _out_kernel,
        grid=(n // tm,),
        in_specs=[rows(gw), rows(gw), rows(gw), rows(gw), mem, mem, rows(4 * gw), rows(d),
                  pl.BlockSpec(w_out.shape, lambda i: (0, 0)),
                  pl.BlockSpec((1, d), lambda i: (0, 0))],
        out_specs=rows(d),
        out_shape=jax.ShapeDtypeStruct((n, d), F32),
        compiler_params=_params("arbitrary"),
        name="mem_attn_gate_out_proj",
    )(oa.reshape(n, gw), ob.reshape(n, gw), oc.reshape(n, gw), cq, mem_k.reshape(b, m, gw),
      mem_v.reshape(b, m, gw), z, x2, w_out, post_g.reshape(1, d))


def _pad_gate_cols(w_in, gate_off, gate_w):
    d = w_in.shape[0]
    return jnp.concatenate([w_in[:, :gate_off + gate_w],
                            jnp.zeros((d, LANES - gate_w), w_in.dtype),
                            w_in[:, gate_off + gate_w:]], axis=1)


def _layer(x2, mem2, b, s, pre_g, post_g, mem_g, w_in, w_mem_kv,
           pe_k, w1_k, w2_k, pe_v, w1_v, w2_v, ret_gn_g, w_out):
    gw = w_out.shape[0] // 4
    kvw = LANES
    gate_w = 3 * 8
    gate_off = 4 * gw + 6 * kvw
    w = _pad_gate_cols(w_in, gate_off, gate_w).astype(BF16)
    widths = (gw, gw, gw, gw, kvw, kvw, 4 * kvw, LANES, gw, gw, gw, 4 * gw)
    dtypes = (BF16, BF16, BF16, BF16, BF16, BF16, BF16, F32, F32, BF16, BF16, BF16)
    (mq, mk, mv, nq, nkc, nvc, nslw, ngate, rqk, rv, cq, z) = _norm_matmul(
        x2, pre_g, w, widths, dtypes, "norm_in_proj")
    mem_k, mem_v = _norm_matmul(mem2, mem_g, w_mem_kv.astype(BF16), (gw, gw), (BF16, BF16),
                                "norm_mem_kv")

    o_moba = _moba(mq, mk, mv, b, s)
    kc, vc = _compress(nkc, nvc, pe_k, w1_k, w2_k, pe_v, w1_v, w2_v, b, s)
    o_nsa = _nsa(nq, ngate, kc, vc, nslw, b, s)
    o_ret = _retention(rqk, rv, ret_gn_g, b, s)
    return _out_proj(o_moba, o_nsa, o_ret, cq, mem_k, mem_v, z, x2, w_out.astype(BF16), post_g, b)


def kernel(x, mem, pre_norm_g, post_norm_g, mem_norm_g, w_in, w_mem_kv, nsa_pe_k, nsa_w1_k, nsa_w2_k,
           nsa_pe_v, nsa_w1_v, nsa_w2_v, ret_gn_g, w_out):
    b, s, d = x.shape
    x2 = x.reshape(b * s, d)
    mem2 = mem.reshape(b * mem.shape[1], d)
    for l in range(w_in.shape[0]):
        x2 = _layer(x2, mem2, b, s, pre_norm_g[l], post_norm_g[l], mem_norm_g[l], w_in[l], w_mem_kv[l],
                    nsa_pe_k[l], nsa_w1_k[l], nsa_w2_k[l], nsa_pe_v[l], nsa_w1_v[l], nsa_w2_v[l],
                    ret_gn_g[l], w_out[l])
    return x2.reshape(b, s, d)
```
